```python
import jax, jax.numpy as jnp
from jax import lax
import numpy as np

D_MODEL = 1024
BATCH = 8
SEQ = 4096
DEPTH = 2

N_MIXERS = 2
EPS = 1e-6
NEG_INF = -1e30
HG_HEADS = 8
HG_DK = 128
HG_DV = D_MODEL // HG_HEADS
HG_CHUNK = 64
NSA_HEADS = 16
NSA_KV = 4
NSA_GROUP = NSA_HEADS // NSA_KV
NSA_HD = D_MODEL // NSA_HEADS
CMP_LEN = 32
CMP_STRIDE = 16
SEL_LEN = 64
SEL_TOPK = 16
WINDOW = 512
Q_BLOCK = 32
FORCED_SCORE = 1e4
D_FF = 2816
CONV_W = 3

kernel_name = 'hybrid_hgrn2_nsa_convffn_adaln'


def rmsnorm(x, g):
    xf = x.astype(jnp.float32)
    y = xf * lax.rsqrt(jnp.mean(xf * xf, axis=-1, keepdims=True) + EPS)
    return (y * g).astype(x.dtype)


def alibi_slopes(n):
    return jnp.asarray([2.0 ** (-8.0 * (h + 1) / n) for h in range(n)], jnp.float32)


def hgrn2_mixer(h, w_in, w_out, g_norm, lb):
    B, S, _ = h.shape
    fd = HG_HEADS * HG_DK
    proj = h @ w_in
    q, f_pre, i_in, g = jnp.split(proj, [fd, 2 * fd, 2 * fd + D_MODEL], axis=-1)
    f_pre = f_pre.astype(jnp.float32)
    log_f = jnp.log(lb + (1.0 - lb) * jax.nn.sigmoid(f_pre))
    k = (1.0 - lb) * jax.nn.sigmoid(-f_pre)
    n = S // HG_CHUNK

    def chunks(t, d):
        return t.astype(jnp.float32).reshape(B, n, HG_CHUNK, HG_HEADS, d).transpose(1, 0, 3, 2, 4)

    xs = (chunks(q, HG_DK), chunks(k, HG_DK), chunks(i_in, HG_DV), chunks(log_f, HG_DK))
    causal = jnp.tril(jnp.ones((HG_CHUNK, HG_CHUNK), bool))[:, :, None]

    def step(state, inp):
        qc, kc, vc, lfc = inp
        b = jnp.cumsum(lfc, axis=2)
        o_inter = jnp.einsum('bhtk,bhkv->bhtv', qc * jnp.exp(b), state)
        diff = b[:, :, :, None, :] - b[:, :, None, :, :]
        decay = jnp.exp(jnp.where(causal, diff, NEG_INF))
        scores = jnp.einsum('bhtk,bhsk,bhtsk->bhts', qc, kc, decay)
        o_intra = jnp.einsum('bhts,bhsv->bhtv', scores, vc)
        b_last = b[:, :, -1:, :]
        state = state * jnp.exp(b_last[:, :, 0, :, None]) + jnp.einsum(
            'bhsk,bhsv->bhkv', kc * jnp.exp(b_last - b), vc)
        return state, o_inter + o_intra

    s0 = jnp.zeros((B, HG_HEADS, HG_DK, HG_DV), jnp.float32)
    _, o = lax.scan(step, s0, xs)
    o = o.transpose(1, 0, 3, 2, 4).reshape(B, S, HG_HEADS, HG_DV)
    o = rmsnorm(o, g_norm).reshape(B, S, D_MODEL)
    o = o * jax.nn.silu(g.astype(jnp.float32))
    return o.astype(h.dtype) @ w_out


def nsa_mixer(h, w_in, w_out, cmp_pe, cmp_w1, cmp_w2):
    B, S, _ = h.shape
    G, HPG, HD = NSA_KV, NSA_GROUP, NSA_HD
    nc = (S - CMP_LEN) // CMP_STRIDE + 1
    ns = S // SEL_LEN
    k_sel_n = min(SEL_TOPK, ns)
    kvd = G * HD
    split_at = [D_MODEL + j * kvd for j in range(7)]
    q, k_c, v_c, k_s, v_s, k_w, v_w, gate = jnp.split(h @ w_in, split_at, axis=-1)
    q = q.reshape(B, S, G, HPG, HD)
    gate = jax.nn.sigmoid(gate.astype(jnp.float32)).reshape(B, S, G, HPG, 3)

    def kv(t):
        return t.reshape(B, S, G, HD)

    cidx = (np.arange(nc) * CMP_STRIDE)[:, None] + np.arange(CMP_LEN)[None, :]

    def compress(t, pe, w1, w2):
        blocks = kv(t)[:, cidx] + pe[None, None, :, None, :]
        blocks = blocks.transpose(0, 1, 3, 2, 4).reshape(B, nc, G, CMP_LEN * HD)
        return jax.nn.silu(blocks @ w1) @ w2

    kc = compress(k_c, cmp_pe[0], cmp_w1[0], cmp_w2[0])
    vc = compress(v_c, cmp_pe[1], cmp_w1[1], cmp_w2[1])
    cmp_end = jnp.asarray(np.arange(nc) * CMP_STRIDE + CMP_LEN - 1, jnp.int32)
    cmp_mid = jnp.asarray(np.arange(nc) * CMP_STRIDE + (CMP_LEN - 1) / 2.0, jnp.float32)
    ci = np.arange(nc)[:, None] * CMP_STRIDE
    sj = np.arange(ns)[None, :] * SEL_LEN
    overlap = jnp.asarray((ci <= sj + SEL_LEN - 1) & (ci + CMP_LEN - 1 >= sj), jnp.float32)

    ks_blk = kv(k_s).reshape(B, ns, SEL_LEN, G, HD).transpose(0, 3, 1, 2, 4)
    vs_blk = kv(v_s).reshape(B, ns, SEL_LEN, G, HD).transpose(0, 3, 1, 2, 4)
    pad = ((0, 0), (WINDOW, 0), (0, 0), (0, 0))
    kw_pad = jnp.pad(kv(k_w), pad)
    vw_pad = jnp.pad(kv(v_w), pad)

    slopes = alibi_slopes(NSA_HEADS).reshape(G, HPG)
    scale = HD ** -0.5
    blk_ids = jnp.arange(ns)
    bi = jnp.arange(B)[:, None, None, None]
    gi = jnp.arange(G)[None, :, None, None]

    def block(i):
        q0 = i * Q_BLOCK
        t = q0 + jnp.arange(Q_BLOCK)
        qb = lax.dynamic_slice_in_dim(q, q0, Q_BLOCK, axis=1)
        gb = lax.dynamic_slice_in_dim(gate, q0, Q_BLOCK, axis=1)
        s_c = jnp.einsum('bqghd,bngd->bghqn', qb, kc, preferred_element_type=jnp.float32) * scale
        dist_c = t[:, None].astype(jnp.float32) - cmp_mid[None, :]
        s_c = s_c - slopes[None, :, :, None, None] * dist_c
        valid_c = cmp_end[None, :] <= t[:, None]
        p_c = jax.nn.softmax(jnp.where(valid_c, s_c, NEG_INF), axis=-1)
        p_c = p_c * (t >= CMP_LEN - 1).astype(jnp.float32)[None, None, None, :, None]
        o_c = jnp.einsum('bghqn,bngd->bqghd', p_c, vc)
        imp = jnp.einsum('bghqn,nj->bgqj', p_c, overlap)
        cur = t // SEL_LEN
        valid_s = blk_ids[None, :] * SEL_LEN <= t[:, None]
        forced = ((blk_ids[None, :] == 0) | (blk_ids[None, :] == cur[:, None])
                  | (blk_ids[None, :] == cur[:, None] - 1)) & valid_s
        score = jnp.where(forced, FORCED_SCORE, jnp.where(valid_s, imp, -1.0))
        _, idx = lax.top_k(score, k_sel_n)
        k_sel = ks_blk[bi, gi, idx]
        v_sel = vs_blk[bi, gi, idx].reshape(B, G, Q_BLOCK, k_sel_n * SEL_LEN, HD)
        pos = idx[..., None] * SEL_LEN + jnp.arange(SEL_LEN)
        s_s = jnp.einsum('bqghd,bgqkld->bghqkl', qb, k_sel, preferred_element_type=jnp.float32) * scale
        dist_s = (t[None, None, :, None, None] - pos).astype(jnp.float32)
        s_s = s_s - slopes[None, :, :, None, None, None] * dist_s[:, :, None]
        mask_s = (pos <= t[None, None, :, None, None])[:, :, None]
        s_s = jnp.where(mask_s, s_s, NEG_INF).reshape(B, G, HPG, Q_BLOCK, k_sel_n * SEL_LEN)
        o_s = jnp.einsum('bghqm,bgqmd->bqghd', jax.nn.softmax(s_s, axis=-1), v_sel)
        kw = lax.dynamic_slice_in_dim(kw_pad, q0, Q_BLOCK + WINDOW, axis=1)
        vw = lax.dynamic_slice_in_dim(vw_pad, q0, Q_BLOCK + WINDOW, axis=1)
        s_pos = q0 - WINDOW + jnp.arange(Q_BLOCK + WINDOW)
        s_w = jnp.einsum('bqghd,bsgd->bghqs', qb, kw, preferred_element_type=jnp.float32) * scale
        s_w = s_w - slopes[None, :, :, None, None] * (t[:, None] - s_pos[None, :]).astype(jnp.float32)
        mask_w = (s_pos[None, :] <= t[:, None]) & (s_pos[None, :] > t[:, None] - WINDOW) & (s_pos[None, :] >= 0)
        o_w = jnp.einsum('bghqs,bsgd->bqghd', jax.nn.softmax(jnp.where(mask_w, s_w, NEG_INF), axis=-1), vw)
        return gb[..., 0, None] * o_c + gb[..., 1, None] * o_s + gb[..., 2, None] * o_w

    o = lax.map(block, jnp.arange(S // Q_BLOCK))
    o = o.transpose(1, 0, 2, 3, 4, 5).reshape(B, S, D_MODEL)
    return o.astype(h.dtype) @ w_out


def conv_ffn(h, w_up, conv_w, conv_b, w_down):
    a, v = jnp.split(h @ w_up, 2, axis=-1)
    a = lax.conv_general_dilated(a, conv_w[:, None, :].astype(a.dtype), window_strides=(1,),
                                 padding=[(CONV_W - 1, 0)], dimension_numbers=('NWC', 'WIO', 'NWC'),
                                 feature_group_count=D_FF) + conv_b
    return (jax.nn.silu(a) * v) @ w_down


def setup_inputs(seed: int = 0) -> dict:
    key = jax.random.key(seed)
    ks = jax.random.split(key, 24)
    n_a = (DEPTH + 1) // 2
    n_b = DEPTH // 2
    D = D_MODEL
    nsa_in = D + 6 * NSA_KV * NSA_HD + 3 * NSA_HEADS

    def nrm(k, shape, s):
        return jax.random.normal(k, shape, jnp.float32) * s

    return {
        'x': nrm(ks[0], (BATCH, SEQ, D), 1.0),
        'c': nrm(ks[1], (BATCH, D), 1.0),
        'ada_w': nrm(ks[2], (DEPTH, D, 6 * D), 0.5 * D ** -0.5),
        'ada_b': nrm(ks[3], (DEPTH, 6 * D), 0.02),
        'norm_mix': 1.0 + nrm(ks[4], (DEPTH, D), 0.02),
        'norm_ffn': 1.0 + nrm(ks[5], (DEPTH, D), 0.02),
        'final_norm': 1.0 + nrm(ks[6], (D,), 0.02),
        'hg_w_in': nrm(ks[7], (n_a, D, 2 * HG_HEADS * HG_DK + 2 * D), D ** -0.5),
        'hg_w_out': nrm(ks[8], (n_a, D, D), D ** -0.5),
        'hg_gnorm': 1.0 + nrm(ks[9], (n_a, HG_DV), 0.02),
        'hg_lb': nrm(ks[10], (n_a + 1, HG_HEADS * HG_DK), 0.5),
        'nsa_w_in': nrm(ks[11], (n_b, D, nsa_in), D ** -0.5),
        'nsa_w_out': nrm(ks[12], (n_b, D, D), D ** -0.5),
        'nsa_cmp_pe': nrm(ks[13], (n_b, 2, CMP_LEN, NSA_HD), 0.1),
        'nsa_cmp_w1': nrm(ks[14], (n_b, 2, CMP_LEN * NSA_HD, NSA_HD), (CMP_LEN * NSA_HD) ** -0.5),
        'nsa_cmp_w2': nrm(ks[15], (n_b, 2, NSA_HD, NSA_HD), NSA_HD ** -0.5),
        'ffn_w_up': nrm(ks[16], (DEPTH, D, 2 * D_FF), D ** -0.5),
        'ffn_conv_w': nrm(ks[17], (DEPTH, CONV_W, D_FF), CONV_W ** -0.5),
        'ffn_conv_b': nrm(ks[18], (DEPTH, D_FF), 0.02),
        'ffn_w_down': nrm(ks[19], (DEPTH, D_FF, D), D_FF ** -0.5),
    }


def reference(x, c, ada_w, ada_b, norm_mix, norm_ffn, final_norm, hg_w_in, hg_w_out, hg_gnorm, hg_lb,
              nsa_w_in, nsa_w_out, nsa_cmp_pe, nsa_cmp_w1, nsa_cmp_w2, ffn_w_up, ffn_conv_w, ffn_conv_b,
              ffn_w_down):
    lb_all = jnp.cumsum(jax.nn.softmax(hg_lb.astype(jnp.float32), axis=0), axis=0)
    c_act = jax.nn.silu(c)
    for layer in range(DEPTH):
        mod = c_act @ ada_w[layer] + ada_b[layer]
        sh1, sc1, g1, sh2, sc2, g2 = [m[:, None, :] for m in jnp.split(mod, 6, axis=-1)]
        hmix = rmsnorm(x, norm_mix[layer]) * (1.0 + sc1) + sh1
        j = layer // N_MIXERS
        if layer % N_MIXERS == 0:
            y = hgrn2_mixer(hmix, hg_w_in[j], hg_w_out[j], hg_gnorm[j], lb_all[j])
        else:
            y = nsa_mixer(hmix, nsa_w_in[j], nsa_w_out[j], nsa_cmp_pe[j], nsa_cmp_w1[j], nsa_cmp_w2[j])
        x = x + g1 * y
        hffn = rmsnorm(x, norm_ffn[layer]) * (1.0 + sc2) + sh2
        x = x + g2 * conv_ffn(hffn, ffn_w_up[layer], ffn_conv_w[layer], ffn_conv_b[layer], ffn_w_down[layer])
    return rmsnorm(x, final_norm)
```

```python
import functools

import numpy as np
import jax
import jax.numpy as jnp
from jax import lax
from jax.experimental import pallas as pl
from jax.experimental.pallas import tpu as pltpu

F32 = jnp.float32
BF16 = jnp.bfloat16
HIGHEST = lax.Precision.HIGHEST

EPS = 1e-6
NEG_INF = -1e30
HG_HEADS = 8
HG_DK = 128
HG_CHUNK = 64
HG_SUB = 8
NSA_HEADS = 16
NSA_KV = 4
NSA_GROUP = NSA_HEADS // NSA_KV
NSA_HD = 64
CMP_LEN = 32
CMP_STRIDE = 16
SEL_LEN = 64
SEL_TOPK = 16
WINDOW = 512
FORCED_SCORE = 1e4
CONV_W = 3
KVW = NSA_KV * NSA_HD

VMEM_LIMIT_BYTES = 56 * 1024 * 1024
FFN_HALO = 16


def _mm(a, b):
    return jnp.dot(a, b, preferred_element_type=F32)


def _nt(a, b):
    return lax.dot_general(a, b, (((1,), (1,)), ((), ())), preferred_element_type=F32)


def _tn(a, b):
    return lax.dot_general(a, b, (((0,), (0,)), ((), ())), preferred_element_type=F32)


def _sigmoid(x):
    return 1.0 / (1.0 + jnp.exp(-x))


def _norm_mod(x, nw, sc, sh):
    ms = jnp.mean(x * x, axis=-1, keepdims=True)
    return (x * lax.rsqrt(ms + EPS) * nw) * (1.0 + sc) + sh


def _params(sem):
    return pltpu.CompilerParams(dimension_semantics=sem, vmem_limit_bytes=VMEM_LIMIT_BYTES)


def _ada_body(c_ref, w_ref, b_ref, o_ref):
    c = c_ref[...]
    ca = c * _sigmoid(c)
    o_ref[0] = jnp.dot(ca, w_ref[0], preferred_element_type=F32, precision=HIGHEST) + b_ref[0]


def _ada(c, ada_w, ada_b):
    depth, d, n6 = ada_w.shape
    bsz = c.shape[0]
    tn = n6 // 4
    return pl.pallas_call(
        _ada_body,
        grid=(depth, n6 // tn),
        in_specs=[pl.BlockSpec((bsz, d), lambda l, j: (0, 0)),
                  pl.BlockSpec((1, d, tn), lambda l, j: (l, 0, j)),
                  pl.BlockSpec((1, 1, tn), lambda l, j: (l, 0, j))],
        out_specs=pl.BlockSpec((1, bsz, tn), lambda l, j: (l, 0, j)),
        out_shape=jax.ShapeDtypeStruct((depth, bsz, n6), F32),
        compiler_params=_params(("arbitrary", "arbitrary")),
        name="ada_mod",
    )(c, ada_w, ada_b.reshape(depth, 1, n6))


def _pair_offset(j, i):
    nb = HG_CHUNK // HG_SUB
    before = sum(nb - 1 - jj for jj in range(j))
    return HG_SUB * (before + (i - j - 1))


def _hgrn_head_chunk(q_s, k_s, lf_s, v_s, b_s, st_s, hidx, r0, ls, tril, ones_r, lane, lane_blk, causal):
    C, SB = HG_CHUNK, HG_SUB
    nb = C // SB
    rows = pl.ds(r0, C)
    q = q_s[rows, ls]
    k = k_s[rows, ls]
    v = v_s[rows, ls]
    lf = lf_s[rows, ls]
    hi = lf.astype(BF16)
    r1 = lf - hi.astype(F32)
    mid = r1.astype(BF16)
    lo = (r1 - mid.astype(F32)).astype(BF16)
    cs = _mm(tril, jnp.concatenate([hi, mid, lo], axis=1))
    b = cs[:, :HG_DK] + cs[:, HG_DK:2 * HG_DK] + cs[:, 2 * HG_DK:]
    b_s[0, :, ls] = b
    b_s[1, :, ls] = k
    st = st_s[hidx]
    qd = (q * jnp.exp(b)).astype(BF16)
    o = _nt(qd, st.astype(BF16))
    bq = [b[SB * i:SB * (i + 1)] for i in range(nb)]
    qq = [q[SB * i:SB * (i + 1)] for i in range(nb)]
    bend = [b_s[0, SB * j + SB - 1:SB * j + SB, ls] for j in range(nb)]
    kt = jnp.concatenate([k[SB * j:SB * (j + 1)] * jnp.exp(bend[j] - bq[j]) for j in range(nb)], axis=0)
    lhs = []
    for j in range(nb - 1):
        for i in range(j + 1, nb):
            lhs.append(qq[i] * jnp.exp(bq[i] - bend[j]))
    res = _nt(jnp.concatenate(lhs, axis=0).astype(BF16), kt.astype(BF16))
    prod = []
    for i in range(nb):
        for s in range(SB):
            r = SB * i + s
            brow = b_s[0, r:r + 1, ls]
            krow = b_s[1, r:r + 1, ls]
            prod.append(qq[i] * krow * jnp.exp(jnp.minimum(bq[i] - brow, 0.0)))
    red = _mm(jnp.concatenate(prod, axis=0).astype(BF16), ones_r)
    srows = []
    for i in range(nb):
        s_i = jnp.zeros((SB, C), F32)
        for j in range(i):
            off = _pair_offset(j, i)
            s_i = jnp.where(lane_blk == j, res[off:off + SB], s_i)
        for s in range(SB):
            r = SB * i + s
            s_i = jnp.where(lane == r, red[SB * r:SB * (r + 1)], s_i)
        srows.append(s_i)
    scores = jnp.where(causal, jnp.concatenate(srows, axis=0), 0.0)
    o = o + _mm(scores.astype(BF16), v.astype(BF16))
    blast = bend[nb - 1]
    kd = k * jnp.exp(blast - b)
    st_s[hidx] = st * jnp.exp(blast) + _tn(v.astype(BF16), kd.astype(BF16))
    return o


def _hgrn_body(lb_row, x_ref, nw_ref, sc_ref, sh_ref, g1_ref, wq_ref, wf_ref, wi_ref, wg_ref, wout_ref,
               gn_ref, lb_ref, o_ref, h_s, q_s, k_s, lf_s, v_s, b_s, oh_s, og_s, st_s):
    si = pl.program_id(1)
    hp = pl.program_id(2)
    ts = x_ref.shape[1]
    width = q_s.shape[1]
    hpb = width // HG_DK
    C = HG_CHUNK

    @pl.when(si == 0)
    def _():
        for hh in range(hpb):
            st_s[hp * hpb + hh] = jnp.zeros((HG_DK, HG_DK), F32)

    @pl.when(hp == 0)
    def _():
        h_s[...] = _norm_mod(x_ref[0], nw_ref[...], sc_ref[0], sh_ref[0]).astype(BF16)

    h = h_s[...]
    lbraw = lb_ref[...]
    e = jnp.exp(lbraw - jnp.max(lbraw, axis=0, keepdims=True))
    lb = jnp.sum(e[:lb_row + 1], axis=0, keepdims=True) / jnp.sum(e, axis=0, keepdims=True)
    q_s[...] = _mm(h, wq_ref[...])
    f = _mm(h, wf_ref[...])
    ef = jnp.exp(-jnp.abs(f))
    rcp = 1.0 / (1.0 + ef)
    pos = f >= 0
    sig_p = jnp.where(pos, rcp, ef * rcp)
    sig_n = jnp.where(pos, ef * rcp, rcp)
    lf_s[...] = jnp.log(lb + (1.0 - lb) * sig_p)
    k_s[...] = (1.0 - lb) * sig_n
    v_s[...] = _mm(h, wi_ref[...])

    ri = lax.broadcasted_iota(jnp.int32, (C, C), 0)
    ci = lax.broadcasted_iota(jnp.int32, (C, C), 1)
    causal = ri >= ci
    tril = jnp.where(causal, 1.0, 0.0).astype(BF16)
    ones_r = jnp.ones((HG_DK, C), BF16)
    lane = lax.broadcasted_iota(jnp.int32, (HG_SUB, C), 1)
    lane_blk = lane // HG_SUB

    def chunk(c, carry):
        r0 = pl.multiple_of(c * C, C)
        for hh in range(hpb):
            ls = slice(hh * HG_DK, (hh + 1) * HG_DK)
            o = _hgrn_head_chunk(q_s, k_s, lf_s, v_s, b_s, st_s, hp * hpb + hh, r0, ls,
                                 tril, ones_r, lane, lane_blk, causal)
            oh_s[pl.ds(r0, C), ls] = o
        return carry

    lax.fori_loop(0, ts // C, chunk, 0)

    g = _mm(h, wg_ref[...])
    gate = g * _sigmoid(g)
    gn = gn_ref[...]
    parts = []
    for hh in range(hpb):
        ls = slice(hh * HG_DK, (hh + 1) * HG_DK)
        oh = oh_s[:, ls]
        ms = jnp.mean(oh * oh, axis=-1, keepdims=True)
        parts.append(oh * lax.rsqrt(ms + EPS) * gn)
    og_s[hp] = (jnp.concatenate(parts, axis=1) * gate).astype(BF16)

    nblk = og_s.shape[0]

    @pl.when(hp == nblk - 1)
    def _():
        y = jnp.zeros(o_ref.shape[1:], F32)
        for blk in range(nblk):
            y = y + _mm(og_s[blk], wout_ref[blk * width:(blk + 1) * width, :])
        o_ref[0] = x_ref[0] + g1_ref[0] * y


def _hgrn_layer(x, nw, sc, sh, g1, w_in, w_out, gnorm, hg_lb, lb_row):
    bsz, seq, d = x.shape
    ts = min(512, seq)
    hpb = 4
    width = hpb * HG_DK
    nblk = d // width
    w_in = w_in.astype(BF16)
    seg = lambda k: pl.BlockSpec((d, width), lambda b, s, hp, k=k: (0, k * nblk + hp))
    vec = pl.BlockSpec((1, 1, d), lambda b, s, hp: (b, 0, 0))
    return pl.pallas_call(
        functools.partial(_hgrn_body, lb_row),
        grid=(bsz, seq // ts, nblk),
        in_specs=[pl.BlockSpec((1, ts, d), lambda b, s, hp: (b, s, 0)),
                  pl.BlockSpec((1, d), lambda b, s, hp: (0, 0)),
                  vec, vec, vec,
                  seg(0), seg(1), seg(2), seg(3),
                  pl.BlockSpec((d, d), lambda b, s, hp: (0, 0)),
                  pl.BlockSpec((1, HG_DK), lambda b, s, hp: (0, 0)),
                  pl.BlockSpec((hg_lb.shape[0], width), lambda b, s, hp: (0, hp))],
        out_specs=pl.BlockSpec((1, ts, d), lambda b, s, hp: (b, s, 0)),
        out_shape=jax.ShapeDtypeStruct(x.shape, F32),
        scratch_shapes=[pltpu.VMEM((ts, d), BF16),
                        pltpu.VMEM((ts, width), F32), pltpu.VMEM((ts, width), F32),
                        pltpu.VMEM((ts, width), F32), pltpu.VMEM((ts, width), F32),
                        pltpu.VMEM((2, HG_CHUNK, width), F32),
                        pltpu.VMEM((ts, width), F32),
                        pltpu.VMEM((nblk, ts, width), BF16),
                        pltpu.VMEM((HG_HEADS, HG_DK, HG_DK), F32)],
        compiler_params=_params(("arbitrary", "arbitrary", "arbitrary")),
        name="hgrn2_layer",
    )(x, nw.reshape(1, d), sc, sh, g1, w_in, w_in, w_in, w_in, w_out.astype(BF16),
      gnorm.reshape(1, HG_DK), hg_lb)


def _ffn_body(final, nchunk, x_ref, xh_ref, nw_ref, sc_ref, sh_ref, g2_ref, wa_ref, wv_ref, cw_ref, cb_ref,
              wd_ref, fn_ref, o_ref, hx_s, a_s):
    i = pl.program_id(1)
    tm = x_ref.shape[1]
    H = FFN_HALO
    x = x_ref[0]
    nw, sc, sh = nw_ref[...], sc_ref[0], sh_ref[0]
    hx_s[H:, :] = _norm_mod(x, nw, sc, sh).astype(BF16)
    hx_s[0:H, :] = jnp.where(i > 0, _norm_mod(xh_ref[0], nw, sc, sh), 0.0).astype(BF16)
    ff = wa_ref.shape[1]
    fc = ff // nchunk
    y = jnp.zeros((tm, x.shape[1]), F32)
    for c in range(nchunk):
        cols = slice(c * fc, (c + 1) * fc)
        a_s[...] = _mm(hx_s[...], wa_ref[:, cols])
        v = _mm(hx_s[H:, :], wv_ref[:, cols])
        cw = cw_ref[:, cols]
        conv = (cw[0:1] * a_s[H - 2:H - 2 + tm, :] + cw[1:2] * a_s[H - 1:H - 1 + tm, :]
                + cw[2:3] * a_s[H:H + tm, :] + cb_ref[:, cols])
        u = (conv * _sigmoid(conv) * v).astype(BF16)
        y = y + _mm(u, wd_ref[cols, :])
    out = x + g2_ref[0] * y
    if final:
        ms = jnp.mean(out * out, axis=-1, keepdims=True)
        out = out * lax.rsqrt(ms + EPS) * fn_ref[...]
    o_ref[0] = out


def _ffn_layer(x, nw, sc, sh, g2, w_up, conv_w, conv_b, w_down, final_norm, final):
    bsz, seq, d = x.shape
    ff = w_down.shape[0]
    tm = min(512, seq)
    nchunk = 2
    H = FFN_HALO
    wa = w_up[:, :ff].astype(BF16)
    wv = w_up[:, ff:].astype(BF16)
    vec = pl.BlockSpec((1, 1, d), lambda b, i: (b, 0, 0))
    full = lambda shape: pl.BlockSpec(shape, lambda b, i: tuple(0 for _ in shape))
    return pl.pallas_call(
        functools.partial(_ffn_body, final, nchunk),
        grid=(bsz, seq // tm),
        in_specs=[pl.BlockSpec((1, tm, d), lambda b, i: (b, i, 0)),
                  pl.BlockSpec((1, H, d), lambda b, i: (b, jnp.maximum(i * (tm // H) - 1, 0), 0)),
                  full((1, d)), vec, vec, vec,
                  full((d, ff)), full((d, ff)), full((CONV_W, ff)), full((1, ff)), full((ff, d)),
                  full((1, d))],
        out_specs=pl.BlockSpec((1, tm, d), lambda b, i: (b, i, 0)),
        out_shape=jax.ShapeDtypeStruct(x.shape, F32),
        scratch_shapes=[pltpu.VMEM((tm + H, d), BF16), pltpu.VMEM((tm + H, ff // nchunk), F32)],
        compiler_params=_params(("arbitrary", "arbitrary")),
        name="conv_ffn_layer",
    )(x, x, nw.reshape(1, d), sc, sh, g2, wa, wv, conv_w, conv_b.reshape(1, ff), w_down.astype(BF16),
      final_norm.reshape(1, d))


def _alibi_slopes():
    return [[2.0 ** (-8.0 * (g * NSA_GROUP + h + 1) / NSA_HEADS) for h in range(NSA_GROUP)]
            for g in range(NSA_KV)]


def _nsa_in_body(x_ref, nw_ref, sc_ref, sh_ref, w_ref, q_ref, kvc_ref, ks_ref, vs_ref, kw_ref, vw_ref, gate_ref):
    d = x_ref.shape[2]
    h = _norm_mod(x_ref[0], nw_ref[...], sc_ref[0], sh_ref[0]).astype(BF16)
    p = _mm(h, w_ref[...])
    q_ref[0] = (p[:, :d] * (NSA_HD ** -0.5)).astype(BF16)
    o = d
    kvc_ref[0] = p[:, o:o + 2 * KVW]
    o += 2 * KVW
    for ref in (ks_ref, vs_ref, kw_ref, vw_ref):
        ref[0] = p[:, o:o + KVW].astype(BF16)
        o += KVW
    gate_ref[0] = _sigmoid(p[:, o:])


def _nsa_in(x, nw, sc, sh, w_cat):
    bsz, seq, d = x.shape
    tm = min(512, seq)
    ncol = w_cat.shape[1]
    tile = lambda w: pl.BlockSpec((1, tm, w), lambda b, i: (b, i, 0))
    vec = pl.BlockSpec((1, 1, d), lambda b, i: (b, 0, 0))
    sds = lambda w, dt: jax.ShapeDtypeStruct((bsz, seq, w), dt)
    return pl.pallas_call(
        _nsa_in_body,
        grid=(bsz, seq // tm),
        in_specs=[tile(d), pl.BlockSpec((1, d), lambda b, i: (0, 0)), vec, vec,
                  pl.BlockSpec((d, ncol), lambda b, i: (0, 0))],
        out_specs=[tile(d), tile(2 * KVW), tile(KVW), tile(KVW), tile(KVW), tile(KVW), tile(128)],
        out_shape=[sds(d, BF16), sds(2 * KVW, F32), sds(KVW, BF16), sds(KVW, BF16), sds(KVW, BF16),
                   sds(KVW, BF16), sds(128, F32)],
        compiler_params=_params(("arbitrary", "arbitrary")),
        name="nsa_in_proj",
    )(x, nw.reshape(1, d), sc, sh, w_cat)


def _cmp_body(kseg_ref, vseg_ref, pe_ref, w1_ref, w2_ref, kc_ref, vc_ref):
    nseg = kseg_ref.shape[2]
    row = lax.broadcasted_iota(jnp.int32, (nseg, NSA_HD), 0)
    for i, (src, dst) in enumerate(((kseg_ref, kc_ref), (vseg_ref, vc_ref))):
        seg = src[0, 0]
        a = _mm((seg + pe_ref[i, 0:1, :]).astype(BF16), w1_ref[i, 0])
        bm = _mm((seg + pe_ref[i, 1:2, :]).astype(BF16), w1_ref[i, 1])
        pre = a + pltpu.roll(bm, nseg - 1, 0)
        act = pre * _sigmoid(pre)
        out = _mm(act.astype(BF16), w2_ref[i])
        dst[0, 0] = jnp.where(row < nseg - 1, out, 0.0)


def _compress(kseg, vseg, pe, w1, w2):
    bsz, ng, nseg, width = kseg.shape
    blk = pl.BlockSpec((1, 1, nseg, width), lambda b, g: (b, g, 0, 0))
    oblk = pl.BlockSpec((1, 1, nseg, NSA_HD), lambda b, g: (b, g, 0, 0))
    full = lambda a: pl.BlockSpec(a.shape, lambda b, g: tuple(0 for _ in a.shape))
    sds = jax.ShapeDtypeStruct((bsz, ng, nseg, NSA_HD), F32)
    return pl.pallas_call(
        _cmp_body,
        grid=(bsz, ng),
        in_specs=[blk, blk, full(pe), full(w1), full(w2)],
        out_specs=[oblk, oblk],
        out_shape=[sds, sds],
        compiler_params=_params(("arbitrary", "arbitrary")),
        name="nsa_compress",
    )(kseg, vseg, pe, w1, w2)


def _sel_body(nsel, q_ref, kc_ref, vc_ref, ovt_ref, oc_ref, selt_ref):
    qi = pl.program_id(1)
    tq = q_ref.shape[1]
    n = kc_ref.shape[1]
    ns = ovt_ref.shape[0]
    q0 = qi * tq
    kc = kc_ref[0]
    vc = vc_ref[0]
    slopes = _alibi_slopes()
    tcol = q0 + lax.broadcasted_iota(jnp.int32, (tq, 1), 0)
    ncol = lax.broadcasted_iota(jnp.int32, (1, n), 1)
    dist = tcol.astype(F32) - (ncol.astype(F32) * CMP_STRIDE + (CMP_LEN - 1) / 2.0)
    valid = (ncol * CMP_STRIDE + (CMP_LEN - 1)) <= tcol
    rowok = jnp.where(tcol >= CMP_LEN - 1, 1.0, 0.0)
    lane = lax.broadcasted_iota(jnp.int32, (1, KVW), 1)
    jidx = lax.broadcasted_iota(jnp.int32, (ns, tq), 0)
    tl = q0 + lax.broadcasted_iota(jnp.int32, (ns, tq), 1)
    cur = tl // SEL_LEN
    valid_s = jidx * SEL_LEN <= tl
    forced = ((jidx == 0) | (jidx == cur) | (jidx == cur - 1)) & valid_s
    oacc = [jnp.zeros((tq, KVW), F32) for _ in range(NSA_GROUP)]
    for g in range(NSA_KV):
        gm = (lane >= g * NSA_HD) & (lane < (g + 1) * NSA_HD)
        psum = jnp.zeros((tq, n), F32)
        for h in range(NSA_GROUP):
            qh = q_ref[0, :, h * KVW:(h + 1) * KVW]
            s = _nt(jnp.where(gm, qh, jnp.zeros_like(qh)), kc)
            s = jnp.where(valid, s - slopes[g][h] * dist, NEG_INF)
            ex = jnp.exp(s - jnp.max(s, axis=-1, keepdims=True))
            p = ex / jnp.sum(ex, axis=-1, keepdims=True) * rowok
            psum = psum + p
            oacc[h] = oacc[h] + jnp.where(gm, _mm(p.astype(BF16), vc), 0.0)
        imp = lax.dot_general(ovt_ref[...], psum, (((1,), (1,)), ((), ())),
                              preferred_element_type=F32, precision=HIGHEST)
        score = jnp.where(forced, FORCED_SCORE, jnp.where(valid_s, imp, -1.0))
        cnt = jnp.zeros((ns, tq), F32)
        for jp in range(ns):
            row = score[jp:jp + 1, :]
            cnt = cnt + jnp.where(jidx > jp, jnp.where(row >= score, 1.0, 0.0), jnp.where(row > score, 1.0, 0.0))
        selt_ref[0, g * ns:(g + 1) * ns, :] = jnp.where(cnt < nsel, 1.0, 0.0)
    oc_ref[0] = jnp.concatenate(oacc, axis=1)


def _select(q, kc, vc, ovt):
    bsz, seq, d = q.shape
    n = kc.shape[1]
    ns = ovt.shape[0]
    tq = min(256, seq)
    nsel = min(SEL_TOPK, ns)
    return pl.pallas_call(
        functools.partial(_sel_body, nsel),
        grid=(bsz, seq // tq),
        in_specs=[pl.BlockSpec((1, tq, d), lambda b, i: (b, i, 0)),
                  pl.BlockSpec((1, n, KVW), lambda b, i: (b, 0, 0)),
                  pl.BlockSpec((1, n, KVW), lambda b, i: (b, 0, 0)),
                  pl.BlockSpec((ns, n), lambda b, i: (0, 0))],
        out_specs=[pl.BlockSpec((1, tq, d), lambda b, i: (b, i, 0)),
                   pl.BlockSpec((1, NSA_KV * ns, tq), lambda b, i: (b, 0, i))],
        out_shape=[jax.ShapeDtypeStruct((bsz, seq, d), F32),
                   jax.ShapeDtypeStruct((bsz, NSA_KV * ns, seq), F32)],
        compiler_params=_params(("arbitrary", "arbitrary")),
        name="nsa_compressed_select",
    )(q, kc, vc, ovt)


def _att_branch(windowed, qg, k_ref, v_ref, lo, hi, tk, selg, e4t_ref, tcol, slopes, m_s, l_s, acc_s):
    tq = tcol.shape[0]
    m_s[...] = jnp.full(m_s.shape, NEG_INF, F32)
    l_s[...] = jnp.zeros(l_s.shape, F32)
    acc_s[...] = jnp.zeros(acc_s.shape, F32)

    def body(kt, carry):
        k0 = pl.multiple_of(kt * tk, tk)
        kb = k_ref[0, pl.ds(k0, tk), :]
        vb = v_ref[0, pl.ds(k0, tk), :]
        s = _nt(qg, kb)
        pos = k0 + lax.broadcasted_iota(jnp.int32, (1, tk), 1)
        dist = (tcol - pos).astype(F32)
        if windowed:
            allowed = (pos <= tcol) & (pos > tcol - WINDOW)
        else:
            picked = _nt(selg, e4t_ref[pl.ds(k0, tk), :])
            allowed = (picked > 0.5) & (pos <= tcol)
        ps = []
        for h in range(NSA_GROUP):
            rows = slice(h * tq, (h + 1) * tq)
            sh = jnp.where(allowed, s[rows] - slopes[h] * dist, NEG_INF)
            m_old = m_s[rows]
            m_new = jnp.maximum(m_old, jnp.max(sh, axis=-1, keepdims=True))
            alpha = jnp.exp(m_old - m_new)
            p = jnp.exp(sh - m_new[:, :1])
            l_s[rows] = alpha * l_s[rows] + jnp.sum(p, axis=-1, keepdims=True)
            m_s[rows] = m_new
            acc_s[rows] = acc_s[rows] * alpha[:, :1]
            ps.append(p.astype(BF16))
        acc_s[...] += _mm(jnp.concatenate(ps, axis=0), vb)
        return carry

    lax.fori_loop(lo, hi, body, 0)
    return acc_s[...] / l_s[:, :1]


def _att_body(tks, q_ref, ks_ref, vs_ref, kw_ref, vw_ref, sel_ref, e4t_ref, oc_ref, gate_ref, gexp_ref,
              x_ref, g1_ref, wout_ref, o_ref, m_s, l_s, acc_s):
    qi = pl.program_id(1)
    tq = q_ref.shape[1]
    q0 = qi * tq
    slopes = _alibi_slopes()
    tcol = q0 + lax.broadcasted_iota(jnp.int32, (tq, 1), 0)
    lane = lax.broadcasted_iota(jnp.int32, (1, KVW), 1)
    sel = sel_ref[0]
    ns = sel.shape[1] // NSA_KV
    lane_sel = lax.broadcasted_iota(jnp.int32, (1, sel.shape[1]), 1)
    o_sel =[jnp.zeros((tq, KVW), F32) for _ in range(NSA_GROUP)]
    o_win = [jnp.zeros((tq, KVW), F32) for _ in range(NSA_GROUP)]
    for g in range(NSA_KV):
        gm = (lane >= g * NSA_HD) & (lane < (g + 1) * NSA_HD)
        qg = jnp.concatenate(
            [jnp.where(gm, q_ref[0, :, h * KVW:(h + 1) * KVW], jnp.zeros((tq, KVW), BF16))
             for h in range(NSA_GROUP)], axis=0)
        selg = jnp.where((lane_sel >= g * ns) & (lane_sel < (g + 1) * ns), sel, jnp.zeros_like(sel))
        hi_s = (q0 + tq + tks - 1) // tks
        osg = _att_branch(False, qg, ks_ref, vs_ref, 0, hi_s, tks, selg, e4t_ref, tcol, slopes[g],
                          m_s, l_s, acc_s)
        for h in range(NSA_GROUP):
            o_sel[h] = o_sel[h] + jnp.where(gm, osg[h * tq:(h + 1) * tq], 0.0)
        lo_w = jnp.maximum(qi - WINDOW // tq, 0)
        owg = _att_branch(True, qg, kw_ref, vw_ref, lo_w, qi + 1, tq, None, None, tcol, slopes[g],
                          m_s, l_s, acc_s)
        for h in range(NSA_GROUP):
            o_win[h] = o_win[h] + jnp.where(gm, owg[h * tq:(h + 1) * tq], 0.0)
    gate = gate_ref[0]
    ghi = gate.astype(BF16)
    glo = (gate - ghi.astype(F32)).astype(BF16)
    branches = (oc_ref[0], jnp.concatenate(o_sel, axis=1), jnp.concatenate(o_win, axis=1))
    o = jnp.zeros(branches[0].shape, F32)
    for c in range(3):
        gc = _mm(ghi, gexp_ref[c]) + _mm(glo, gexp_ref[c])
        o = o + gc * branches[c]
    y = _mm(o.astype(BF16), wout_ref[...])
    o_ref[0] = x_ref[0] + g1_ref[0] * y


def _attend(q, ks, vs, kw, vw, sel, e4t, oc, gate, gexp, x, g1, w_out):
    bsz, seq, d = x.shape
    tq = min(256, seq)
    tks = min(512, seq)
    tile = lambda w: pl.BlockSpec((1, tq, w), lambda b, i: (b, i, 0))
    kv = pl.BlockSpec((1, seq, KVW), lambda b, i: (b, 0, 0))
    full = lambda a: pl.BlockSpec(a.shape, lambda b, i: tuple(0 for _ in a.shape))
    return pl.pallas_call(
        functools.partial(_att_body, tks),
        grid=(bsz, seq // tq),
        in_specs=[tile(d), kv, kv, kv, kv, tile(sel.shape[2]), full(e4t), tile(d), tile(128), full(gexp),
                  tile(d), pl.BlockSpec((1, 1, d), lambda b, i: (b, 0, 0)), full(w_out)],
        out_specs=tile(d),
        out_shape=jax.ShapeDtypeStruct(x.shape, F32),
        scratch_shapes=[pltpu.VMEM((NSA_GROUP * tq, 128), F32), pltpu.VMEM((NSA_GROUP * tq, 128), F32),
                        pltpu.VMEM((NSA_GROUP * tq, KVW), F32)],
        compiler_params=_params(("arbitrary", "arbitrary")),
        name="nsa_attend",
    )(q, ks, vs, kw, vw, sel, e4t, oc, gate, gexp, x, g1, w_out)


def _nsa_layer(x, nw, sc, sh, g1, w_in, w_out, cmp_pe, cmp_w1, cmp_w2):
    bsz, seq, d = x.shape
    G, HPG, HD = NSA_KV, NSA_GROUP, NSA_HD
    wq = w_in[:, :d].reshape(d, G, HPG, HD).transpose(0, 2, 1, 3).reshape(d, d)
    ngate = 3 * NSA_HEADS
    wgate = jnp.pad(w_in[:, d + 6 * KVW:], ((0, 0), (0, 128 - ngate)))
    w_cat = jnp.concatenate([wq, w_in[:, d:d + 6 * KVW], wgate], axis=1).astype(BF16)
    w_out_p = w_out.reshape(G, HPG, HD, d).transpose(1, 0, 2, 3).reshape(d, d).astype(BF16)

    q, kvc, ks, vs, kw, vw, gate = _nsa_in(x, nw, sc, sh, w_cat)

    nseg = seq // CMP_STRIDE

    def strides(t):
        return t.reshape(bsz, nseg, CMP_STRIDE, G, HD).transpose(0, 3, 1, 2, 4).reshape(
            bsz, G, nseg, CMP_STRIDE * HD)

    pe = cmp_pe.reshape(2, 2, CMP_STRIDE * HD)
    w1 = cmp_w1.reshape(2, 2, CMP_STRIDE * HD, HD).astype(BF16)
    kc, vc = _compress(strides(kvc[..., :KVW]), strides(kvc[..., KVW:]), pe, w1, cmp_w2.astype(BF16))
    lanes = lambda t: t.transpose(0, 2, 1, 3).reshape(bsz, nseg, KVW).astype(BF16)

    ns = seq // SEL_LEN
    ci = np.arange(nseg)[:, None] * CMP_STRIDE
    sj = np.arange(ns)[None, :] * SEL_LEN
    overlap = ((ci <= sj + SEL_LEN - 1) & (ci + CMP_LEN - 1 >= sj)
               & (np.arange(nseg)[:, None] < nseg - 1))
    ovt = jnp.asarray(overlap.T, F32)
    oc, selt = _select(q, lanes(kc), lanes(vc), ovt)
    sel = selt.transpose(0, 2, 1).astype(BF16)

    key_blk = np.arange(seq)[:, None] // SEL_LEN
    e4t = jnp.asarray(key_blk == (np.arange(G * ns)[None, :] % ns), BF16)
    gexp = np.zeros((3, 128, d), np.float32)
    for g in range(G):
        for h in range(HPG):
            for c in range(3):
                gexp[c, g * HPG * 3 + h * 3 + c, (h * G + g) * HD:(h * G + g + 1) * HD] = 1.0
    return _attend(q, ks, vs, kw, vw, sel, e4t, oc, gate, jnp.asarray(gexp, BF16), x, g1, w_out_p)


def kernel(x, c, ada_w, ada_b, norm_mix, norm_ffn, final_norm, hg_w_in, hg_w_out, hg_gnorm, hg_lb,
           nsa_w_in, nsa_w_out, nsa_cmp_pe, nsa_cmp_w1, nsa_cmp_w2, ffn_w_up, ffn_conv_w, ffn_conv_b,
           ffn_w_down):
    depth = ada_w.shape[0]
    d = x.shape[-1]
    mod = _ada(c, ada_w, ada_b)
    for layer in range(depth):
        sh1, sc1, g1, sh2, sc2, g2 = [mod[layer, :, None, k * d:(k + 1) * d] for k in range(6)]
        j = layer // 2
        if layer % 2 == 0:
            x = _hgrn_layer(x, norm_mix[layer], sc1, sh1, g1, hg_w_in[j], hg_w_out[j], hg_gnorm[j], hg_lb, j)
        else:
            x = _nsa_layer(x, norm_mix[layer], sc1, sh1, g1, nsa_w_in[j], nsa_w_out[j], nsa_cmp_pe[j],
                           nsa_cmp_w1[j], nsa_cmp_w2[j])
        x = _ffn_layer(x, norm_ffn[layer], sc2, sh2, g2, ffn_w_up[layer], ffn_conv_w[layer],
                       ffn_conv_b[layer], ffn_w_down[layer], final_norm, layer == depth - 1)
    return x
```

```python
import functools

import numpy as np
import jax
import jax.numpy as jnp
from jax import lax
from jax.experimental import pallas as pl
from jax.experimental.pallas import tpu as pltpu

F32 = jnp.float32
BF16 = jnp.bfloat16
HIGHEST = lax.Precision.HIGHEST

EPS = 1e-6
NEG_INF = -1e30
HG_HEADS = 8
HG_DK = 128
HG_CHUNK = 64
HG_SUB = 8
NSA_HEADS = 16
NSA_KV = 4
NSA_GROUP = NSA_HEADS // NSA_KV
NSA_HD = 64
CMP_LEN = 32
CMP_STRIDE = 16
SEL_LEN = 64
SEL_TOPK = 16
WINDOW = 512
FORCED_SCORE = 1e4
CONV_W = 3
KVW = NSA_KV * NSA_HD

VMEM_LIMIT_BYTES = 56 * 1024 * 1024
FFN_HALO = 16


def _mm(a, b):
    return jnp.dot(a, b, preferred_element_type=F32)


def _nt(a, b):
    return lax.dot_general(a, b, (((1,), (1,)), ((), ())), preferred_element_type=F32)


def _tn(a, b):
    return lax.dot_general(a, b, (((0,), (0,)), ((), ())), preferred_element_type=F32)


def _sigmoid(x):
    return 1.0 / (1.0 + jnp.exp(-x))


def _norm_mod(x, nw, sc, sh):
    ms = jnp.mean(x * x, axis=-1, keepdims=True)
    return (x * lax.rsqrt(ms + EPS) * nw) * (1.0 + sc) + sh


def _params(sem):
    return pltpu.CompilerParams(dimension_semantics=sem, vmem_limit_bytes=VMEM_LIMIT_BYTES)


def _ada_body(c_ref, w_ref, b_ref, o_ref):
    c = c_ref[...]
    ca = c * _sigmoid(c)
    o_ref[0] = jnp.dot(ca, w_ref[0], preferred_element_type=F32, precision=HIGHEST) + b_ref[0]


def _ada(c, ada_w, ada_b):
    depth, d, n6 = ada_w.shape
    bsz = c.shape[0]
    tn = n6 // 4
    return pl.pallas_call(
        _ada_body,
        grid=(depth, n6 // tn),
        in_specs=[pl.BlockSpec((bsz, d), lambda l, j: (0, 0)),
                  pl.BlockSpec((1, d, tn), lambda l, j: (l, 0, j)),
                  pl.BlockSpec((1, 1, tn), lambda l, j: (l, 0, j))],
        out_specs=pl.BlockSpec((1, bsz, tn), lambda l, j: (l, 0, j)),
        out_shape=jax.ShapeDtypeStruct((depth, bsz, n6), F32),
        compiler_params=_params(("arbitrary", "arbitrary")),
        name="ada_mod",
    )(c, ada_w, ada_b.reshape(depth, 1, n6))


def _pair_offset(j, i):
    nb = HG_CHUNK // HG_SUB
    before = sum(nb - 1 - jj for jj in range(j))
    return HG_SUB * (before + (i - j - 1))


def _hgrn_head_chunk(q_s, k_s, lf_s, v_s, b_s, st_s, hidx, r0, ls, tril, ones_r, lane, lane_blk, causal):
    C, SB = HG_CHUNK, HG_SUB
    nb = C // SB
    rows = pl.ds(r0, C)
    q = q_s[rows, ls]
    k = k_s[rows, ls]
    v = v_s[rows, ls]
    lf = lf_s[rows, ls]
    hi = lf.astype(BF16)
    r1 = lf - hi.astype(F32)
    mid = r1.astype(BF16)
    lo = (r1 - mid.astype(F32)).astype(BF16)
    cs = _mm(tril, jnp.concatenate([hi, mid, lo], axis=1))
    b = cs[:, :HG_DK] + cs[:, HG_DK:2 * HG_DK] + cs[:, 2 * HG_DK:]
    b_s[0, :, ls] = b
    b_s[1, :, ls] = k
    st = st_s[hidx]
    qd = (q * jnp.exp(b)).astype(BF16)
    o = _nt(qd, st.astype(BF16))
    bq = [b[SB * i:SB * (i + 1)] for i in range(nb)]
    qq = [q[SB * i:SB * (i + 1)] for i in range(nb)]
    bend = [b_s[0, SB * j + SB - 1:SB * j + SB, ls] for j in range(nb)]
    kt = jnp.concatenate([k[SB * j:SB * (j + 1)] * jnp.exp(bend[j] - bq[j]) for j in range(nb)], axis=0)
    lhs = []
    for j in range(nb - 1):
        for i in range(j + 1, nb):
            lhs.append(qq[i] * jnp.exp(bq[i] - bend[j]))
    res = _nt(jnp.concatenate(lhs, axis=0).astype(BF16), kt.astype(BF16))
    prod = []
    for i in range(nb):
        for s in range(SB):
            r = SB * i + s
            brow = b_s[0, r:r + 1, ls]
            krow = b_s[1, r:r + 1, ls]
            prod.append(qq[i] * krow * jnp.exp(jnp.minimum(bq[i] - brow, 0.0)))
    red = _mm(jnp.concatenate(prod, axis=0).astype(BF16), ones_r)
    srows = []
    for i in range(nb):
        s_i = jnp.zeros((SB, C), F32)
        for j in range(i):
            off = _pair_offset(j, i)
            s_i = jnp.where(lane_blk == j, res[off:off + SB], s_i)
        for s in range(SB):
            r = SB * i + s
            s_i = jnp.where(lane == r, red[SB * r:SB * (r + 1)], s_i)
        srows.append(s_i)
    scores = jnp.where(causal, jnp.concatenate(srows, axis=0), 0.0)
    o = o + _mm(scores.astype(BF16), v.astype(BF16))
    blast = bend[nb - 1]
    kd = k * jnp.exp(blast - b)
    st_s[hidx] = st * jnp.exp(blast) + _tn(v.astype(BF16), kd.astype(BF16))
    return o


def _hgrn_body(lb_row, x_ref, nw_ref, sc_ref, sh_ref, g1_ref, wq_ref, wf_ref, wi_ref, wg_ref, wout_ref,
               gn_ref, lb_ref, o_ref, h_s, q_s, k_s, lf_s, v_s, b_s, oh_s, og_s, st_s):
    si = pl.program_id(1)
    hp = pl.program_id(2)
    ts = x_ref.shape[1]
    width = q_s.shape[1]
    hpb = width // HG_DK
    C = HG_CHUNK

    @pl.when(si == 0)
    def _():
        for hh in range(hpb):
            st_s[hp * hpb + hh] = jnp.zeros((HG_DK, HG_DK), F32)

    @pl.when(hp == 0)
    def _():
        h_s[...] = _norm_mod(x_ref[0], nw_ref[...], sc_ref[0], sh_ref[0]).astype(BF16)

    h = h_s[...]
    lbraw = lb_ref[...]
    e = jnp.exp(lbraw - jnp.max(lbraw, axis=0, keepdims=True))
    lb = jnp.sum(e[:lb_row + 1], axis=0, keepdims=True) / jnp.sum(e, axis=0, keepdims=True)
    q_s[...] = _mm(h, wq_ref[...])
    f = _mm(h, wf_ref[...])
    ef = jnp.exp(-jnp.abs(f))
    rcp = 1.0 / (1.0 + ef)
    pos = f >= 0
    sig_p = jnp.where(pos, rcp, ef * rcp)
    sig_n = jnp.where(pos, ef * rcp, rcp)
    lf_s[...] = jnp.log(lb + (1.0 - lb) * sig_p)
    k_s[...] = (1.0 - lb) * sig_n
    v_s[...] = _mm(h, wi_ref[...])

    ri = lax.broadcasted_iota(jnp.int32, (C, C), 0)
    ci = lax.broadcasted_iota(jnp.int32, (C, C), 1)
    causal = ri >= ci
    tril = jnp.where(causal, 1.0, 0.0).astype(BF16)
    ones_r = jnp.ones((HG_DK, C), BF16)
    lane = lax.broadcasted_iota(jnp.int32, (HG_SUB, C), 1)
    lane_blk = lane // HG_SUB

    def chunk(c, carry):
        r0 = pl.multiple_of(c * C, C)
        for hh in range(hpb):
            ls = slice(hh * HG_DK, (hh + 1) * HG_DK)
            o = _hgrn_head_chunk(q_s, k_s, lf_s, v_s, b_s, st_s, hp * hpb + hh, r0, ls,
                                 tril, ones_r, lane, lane_blk, causal)
            oh_s[pl.ds(r0, C), ls] = o
        return carry

    lax.fori_loop(0, ts // C, chunk, 0)

    g = _mm(h, wg_ref[...])
    gate = g * _sigmoid(g)
    gn = gn_ref[...]
    parts = []
    for hh in range(hpb):
        ls = slice(hh * HG_DK, (hh + 1) * HG_DK)
        oh = oh_s[:, ls]
        ms = jnp.mean(oh * oh, axis=-1, keepdims=True)
        parts.append(oh * lax.rsqrt(ms + EPS) * gn)
    og_s[hp] = (jnp.concatenate(parts, axis=1) * gate).astype(BF16)

    nblk = og_s.shape[0]

    @pl.when(hp == nblk - 1)
    def _():
        y = jnp.zeros(o_ref.shape[1:], F32)
        for blk in range(nblk):
            y = y + _mm(og_s[blk], wout_ref[blk * width:(blk + 1) * width, :])
        o_ref[0] = x_ref[0] + g1_ref[0] * y


def _hgrn_layer(x, nw, sc, sh, g1, w_in, w_out, gnorm, hg_lb, lb_row):
    bsz, seq, d = x.shape
    ts = min(512, seq)
    hpb = 4
    width = hpb * HG_DK
    nblk = d // width
    w_in = w_in.astype(BF16)
    seg = lambda k: pl.BlockSpec((d, width), lambda b, s, hp, k=k: (0, k * nblk + hp))
    vec = pl.BlockSpec((1, 1, d), lambda b, s, hp: (b, 0, 0))
    return pl.pallas_call(
        functools.partial(_hgrn_body, lb_row),
        grid=(bsz, seq // ts, nblk),
        in_specs=[pl.BlockSpec((1, ts, d), lambda b, s, hp: (b, s, 0)),
                  pl.BlockSpec((1, d), lambda b, s, hp: (0, 0)),
                  vec, vec, vec,
                  seg(0), seg(1), seg(2), seg(3),
                  pl.BlockSpec((d, d), lambda b, s, hp: (0, 0)),
                  pl.BlockSpec((1, HG_DK), lambda b, s, hp: (0, 0)),
                  pl.BlockSpec((hg_lb.shape[0], width), lambda b, s, hp: (0, hp))],
        out_specs=pl.BlockSpec((1, ts, d), lambda b, s, hp: (b, s, 0)),
        out_shape=jax.ShapeDtypeStruct(x.shape, F32),
        scratch_shapes=[pltpu.VMEM((ts, d), BF16),
                        pltpu.VMEM((ts, width), F32), pltpu.VMEM((ts, width), F32),
                        pltpu.VMEM((ts, width), F32), pltpu.VMEM((ts, width), F32),
                        pltpu.VMEM((2, HG_CHUNK, width), F32),
                        pltpu.VMEM((ts, width), F32),
                        pltpu.VMEM((nblk, ts, width), BF16),
                        pltpu.VMEM((HG_HEADS, HG_DK, HG_DK), F32)],
        compiler_params=_params(("arbitrary", "arbitrary", "arbitrary")),
        name="hgrn2_layer",
    )(x, nw.reshape(1, d), sc, sh, g1, w_in, w_in, w_in, w_in, w_out.astype(BF16),
      gnorm.reshape(1, HG_DK), hg_lb)


def _ffn_body(final, nchunk, x_ref, xh_ref, nw_ref, sc_ref, sh_ref, g2_ref, wa_ref, wv_ref, cw_ref, cb_ref,
              wd_ref, fn_ref, o_ref, hx_s, a_s):
    i = pl.program_id(1)
    tm = x_ref.shape[1]
    H = FFN_HALO
    x = x_ref[0]
    nw, sc, sh = nw_ref[...], sc_ref[0], sh_ref[0]
    hx_s[H:, :] = _norm_mod(x, nw, sc, sh).astype(BF16)
    hx_s[0:H, :] = jnp.where(i > 0, _norm_mod(xh_ref[0], nw, sc, sh), 0.0).astype(BF16)
    ff = wa_ref.shape[1]
    fc = ff // nchunk
    y = jnp.zeros((tm, x.shape[1]), F32)
    for c in range(nchunk):
        cols = slice(c * fc, (c + 1) * fc)
        a_s[...] = _mm(hx_s[...], wa_ref[:, cols])
        v = _mm(hx_s[H:, :], wv_ref[:, cols])
        cw = cw_ref[:, cols]
        conv = (cw[0:1] * a_s[H - 2:H - 2 + tm, :] + cw[1:2] * a_s[H - 1:H - 1 + tm, :]
                + cw[2:3] * a_s[H:H + tm, :] + cb_ref[:, cols])
        u = (conv * _sigmoid(conv) * v).astype(BF16)
        y = y + _mm(u, wd_ref[cols, :])
    out = x + g2_ref[0] * y
    if final:
        ms = jnp.mean(out * out, axis=-1, keepdims=True)
        out = out * lax.rsqrt(ms + EPS) * fn_ref[...]
    o_ref[0] = out


def _ffn_layer(x, nw, sc, sh, g2, w_up, conv_w, conv_b, w_down, final_norm, final):
    bsz, seq, d = x.shape
    ff = w_down.shape[0]
    tm = min(512, seq)
    nchunk = 2
    H = FFN_HALO
    wa = w_up[:, :ff].astype(BF16)
    wv = w_up[:, ff:].astype(BF16)
    vec = pl.BlockSpec((1, 1, d), lambda b, i: (b, 0, 0))
    full = lambda shape: pl.BlockSpec(shape, lambda b, i: tuple(0 for _ in shape))
    return pl.pallas_call(
        functools.partial(_ffn_body, final, nchunk),
        grid=(bsz, seq // tm),
        in_specs=[pl.BlockSpec((1, tm, d), lambda b, i: (b, i, 0)),
                  pl.BlockSpec((1, H, d), lambda b, i: (b, jnp.maximum(i * (tm // H) - 1, 0), 0)),
                  full((1, d)), vec, vec, vec,
                  full((d, ff)), full((d, ff)), full((CONV_W, ff)), full((1, ff)), full((ff, d)),
                  full((1, d))],
        out_specs=pl.BlockSpec((1, tm, d), lambda b, i: (b, i, 0)),
        out_shape=jax.ShapeDtypeStruct(x.shape, F32),
        scratch_shapes=[pltpu.VMEM((tm + H, d), BF16), pltpu.VMEM((tm + H, ff // nchunk), F32)],
        compiler_params=_params(("arbitrary", "arbitrary")),
        name="conv_ffn_layer",
    )(x, x, nw.reshape(1, d), sc, sh, g2, wa, wv, conv_w, conv_b.reshape(1, ff), w_down.astype(BF16),
      final_norm.reshape(1, d))


def _alibi_slopes():
    return [[2.0 ** (-8.0 * (g * NSA_GROUP + h + 1) / NSA_HEADS) for h in range(NSA_GROUP)]
            for g in range(NSA_KV)]


def _slot(lane, g):
    return (lane >= g * NSA_HD) & (lane < (g + 1) * NSA_HD)


def _nsa_in_body(x_ref, nw_ref, sc_ref, sh_ref, w_ref, fks_ref, fkw_ref, fv_ref,
                 q_ref, kvc_ref, ks_ref, vs_ref, kw_ref, vw_ref, gate_ref):
    d = x_ref.shape[2]
    h = _norm_mod(x_ref[0], nw_ref[...], sc_ref[0], sh_ref[0]).astype(BF16)
    p = _mm(h, w_ref[...])
    q_ref[0] = (p[:, :d] * (NSA_HD ** -0.5)).astype(BF16)
    o = d
    kvc_ref[0] = p[:, o:o + 2 * KVW]
    o += 2 * KVW
    ksb, vsb, kwb, vwb = [p[:, o + i * KVW:o + (i + 1) * KVW].astype(BF16) for i in range(4)]
    o += 4 * KVW
    lane = lax.broadcasted_iota(jnp.int32, (1, KVW), 1)
    lane_v = lax.broadcasted_iota(jnp.int32, (1, 2 * NSA_HD), 1)
    for g in range(NSA_KV):
        own = _slot(lane, g)
        ks_ref[0, g] = jnp.where(own, ksb, fks_ref[g])
        kw_ref[0, g] = jnp.where(own, kwb, fkw_ref[g])
        half = slice((g // 2) * 2 * NSA_HD, (g // 2 + 1) * 2 * NSA_HD)
        own_v = _slot(lane_v, g % 2)
        fv = fv_ref[g].astype(BF16)
        vs_ref[0, g] = jnp.where(own_v, vsb[:, half], fv)
        vw_ref[0, g] = jnp.where(own_v, vwb[:, half], fv)
    gate_ref[0] = _sigmoid(p[:, o:])


def _nsa_in(x, nw, sc, sh, w_cat, fks, fkw, fv):
    bsz, seq, d = x.shape
    tm = min(512, seq)
    ncol = w_cat.shape[1]
    G = NSA_KV
    tile = lambda w: pl.BlockSpec((1, tm, w), lambda b, i: (b, i, 0))
    gtile = lambda w: pl.BlockSpec((1, G, tm, w), lambda b, i: (b, 0, i, 0))
    vec = pl.BlockSpec((1, 1, d), lambda b, i: (b, 0, 0))
    sds = lambda w, dt: jax.ShapeDtypeStruct((bsz, seq, w), dt)
    gsds = lambda w: jax.ShapeDtypeStruct((bsz, G, seq, w), BF16)
    feat = pl.BlockSpec((G, tm, KVW), lambda b, i: (0, i, 0))
    return pl.pallas_call(
        _nsa_in_body,
        grid=(bsz, seq // tm),
        in_specs=[tile(d), pl.BlockSpec((1, d), lambda b, i: (0, 0)), vec, vec,
                  pl.BlockSpec((d, ncol), lambda b, i: (0, 0)), feat, feat,
                  pl.BlockSpec((G, 1, 2 * NSA_HD), lambda b, i: (0, 0, 0))],
        out_specs=[tile(d), tile(2 * KVW), gtile(KVW), gtile(2 * NSA_HD), gtile(KVW), gtile(2 * NSA_HD),
                   tile(128)],
        out_shape=[sds(d, BF16), sds(2 * KVW, F32), gsds(KVW), gsds(2 * NSA_HD), gsds(KVW),
                   gsds(2 * NSA_HD), sds(128, F32)],
        compiler_params=_params(("arbitrary", "arbitrary")),
        name="nsa_in_proj",
    )(x, nw.reshape(1, d), sc, sh, w_cat, fks, fkw, fv)


def _cmp_body(kseg_ref, vseg_ref, pe_ref, w1_ref, w2_ref, kc_ref, vc_ref):
    nseg = kseg_ref.shape[2]
    row = lax.broadcasted_iota(jnp.int32, (nseg, NSA_HD), 0)
    for i, (src, dst) in enumerate(((kseg_ref, kc_ref), (vseg_ref, vc_ref))):
        seg = src[0, 0]
        a = _mm((seg + pe_ref[i, 0:1, :]).astype(BF16), w1_ref[i, 0])
        bm = _mm((seg + pe_ref[i, 1:2, :]).astype(BF16), w1_ref[i, 1])
        pre = a + pltpu.roll(bm, nseg - 1, 0)
        act = pre * _sigmoid(pre)
        out = _mm(act.astype(BF16), w2_ref[i])
        dst[0, 0] = jnp.where(row < nseg - 1, out, 0.0)


def _compress(kseg, vseg, pe, w1, w2):
    bsz, ng, nseg, width = kseg.shape
    blk = pl.BlockSpec((1, 1, nseg, width), lambda b, g: (b, g, 0, 0))
    oblk = pl.BlockSpec((1, 1, nseg, NSA_HD), lambda b, g: (b, g, 0, 0))
    full = lambda a: pl.BlockSpec(a.shape, lambda b, g: tuple(0 for _ in a.shape))
    sds = jax.ShapeDtypeStruct((bsz, ng, nseg, NSA_HD), F32)
    return pl.pallas_call(
        _cmp_body,
        grid=(bsz, ng),
        in_specs=[blk, blk, full(pe), full(w1), full(w2)],
        out_specs=[oblk, oblk],
        out_shape=[sds, sds],
        compiler_params=_params(("arbitrary", "arbitrary")),
        name="nsa_compress",
    )(kseg, vseg, pe, w1, w2)


def _sel_body(nsel, q_ref, kc_ref, vc_ref, ovt_ref, oc_ref, selt_ref):
    qi = pl.program_id(1)
    tq = q_ref.shape[1]
    n = kc_ref.shape[1]
    ns = ovt_ref.shape[0]
    q0 = qi * tq
    kc = kc_ref[0]
    vc = vc_ref[0]
    slopes = _alibi_slopes()
    tcol = q0 + lax.broadcasted_iota(jnp.int32, (tq, 1), 0)
    ncol = lax.broadcasted_iota(jnp.int32, (1, n), 1)
    dist = tcol.astype(F32) - (ncol.astype(F32) * CMP_STRIDE + (CMP_LEN - 1) / 2.0)
    valid = (ncol * CMP_STRIDE + (CMP_LEN - 1)) <= tcol
    rowok = jnp.where(tcol >= CMP_LEN - 1, 1.0, 0.0)
    lane = lax.broadcasted_iota(jnp.int32, (1, KVW), 1)
    jidx = lax.broadcasted_iota(jnp.int32, (ns, tq), 0)
    tl = q0 + lax.broadcasted_iota(jnp.int32, (ns, tq), 1)
    cur = tl // SEL_LEN
    valid_s = jidx * SEL_LEN <= tl
    forced = ((jidx == 0) | (jidx == cur) | (jidx == cur - 1)) & valid_s
    oacc = [jnp.zeros((tq, KVW), F32) for _ in range(NSA_GROUP)]
    for g in range(NSA_KV):
        gm = (lane >= g * NSA_HD) & (lane < (g + 1) * NSA_HD)
        psum = jnp.zeros((tq, n), F32)
        for h in range(NSA_GROUP):
            qh = q_ref[0, :, h * KVW:(h + 1) * KVW]
            s = _nt(jnp.where(gm, qh, jnp.zeros_like(qh)), kc)
            s = jnp.where(valid, s - slopes[g][h] * dist, NEG_INF)
            ex = jnp.exp(s - jnp.max(s, axis=-1, keepdims=True))
            p = ex / jnp.sum(ex, axis=-1, keepdims=True) * rowok
            psum = psum + p
            oacc[h] = oacc[h] + jnp.where(gm, _mm(p.astype(BF16), vc), 0.0)
        imp = lax.dot_general(ovt_ref[...], psum, (((1,), (1,)), ((), ())),
                              preferred_element_type=F32, precision=HIGHEST)
        score = jnp.where(forced, FORCED_SCORE, jnp.where(valid_s, imp, -1.0))
        cnt = jnp.zeros((ns, tq), F32)
        for jp in range(ns):
            row = score[jp:jp + 1, :]
            cnt = cnt + jnp.where(jidx > jp, jnp.where(row >= score, 1.0, 0.0), jnp.where(row > score, 1.0, 0.0))
        r0 = ((g + 1) % NSA_KV) * NSA_HD
        selt_ref[0, r0:r0 + ns, :] = jnp.where(cnt < nsel, 0.0, NEG_INF)
        if ns < NSA_HD:
            selt_ref[0, r0 + ns:r0 + NSA_HD, :] = jnp.zeros((NSA_HD - ns, tq), F32)
    oc_ref[0] = jnp.concatenate(oacc, axis=1)


def _select(q, kc, vc, ovt):
    bsz, seq, d = q.shape
    n = kc.shape[1]
    ns = ovt.shape[0]
    tq = min(256, seq)
    nsel = min(SEL_TOPK, ns)
    return pl.pallas_call(
        functools.partial(_sel_body, nsel),
        grid=(bsz, seq // tq),
        in_specs=[pl.BlockSpec((1, tq, d), lambda b, i: (b, i, 0)),
                  pl.BlockSpec((1, n, KVW), lambda b, i: (b, 0, 0)),
                  pl.BlockSpec((1, n, KVW), lambda b, i: (b, 0, 0)),
                  pl.BlockSpec((ns, n), lambda b, i: (0, 0))],
        out_specs=[pl.BlockSpec((1, tq, d), lambda b, i: (b, i, 0)),
                   pl.BlockSpec((1, KVW, tq), lambda b, i: (b, 0, i))],
        out_shape=[jax.ShapeDtypeStruct((bsz, seq, d), F32),
                   jax.ShapeDtypeStruct((bsz, KVW, seq), F32)],
        compiler_params=_params(("arbitrary", "arbitrary")),
        name="nsa_compressed_select",
    )(q, kc, vc, ovt)


def _softmax_tile(qg, kb, vb, mask, tq, m_s, acc_s):
    s = _nt(qg, kb)
    tk = kb.shape[0]
    ps = []
    for h in range(NSA_GROUP):
        rows = slice(h * tq, (h + 1) * tq)
        sh = s[rows]
        if mask is not None:
            sh = jnp.where(mask, sh, NEG_INF)
        m_old = m_s[rows]
        m_new = jnp.maximum(m_old, jnp.max(sh, axis=-1, keepdims=True))
        p = jnp.exp(sh - jnp.concatenate([m_new] * (tk // 128), axis=1))
        m_s[rows] = m_new
        acc_s[rows] = acc_s[rows] * jnp.exp(m_old - m_new)
        ps.append(p.astype(BF16))
    acc_s[...] += _mm(jnp.concatenate(ps, axis=0), vb)


def _finish(g, acc_s):
    one = ((g + 1) % 2) * NSA_HD
    acc = acc_s[...]
    return acc * (1.0 / acc[:, one:one + 1])


def _att_body(tks, q_ref, ks_ref, vs_ref, kw_refs, vw_refs, selm_ref, sf_ref, oc_ref, gate_ref, gexp_ref,
              x_ref, g1_ref, wout_ref, o_ref, m_s, acc_s):
    qi = pl.program_id(1)
    tq = q_ref.shape[1]
    q0 = qi * tq
    nwt = len(kw_refs) - 1
    lane = lax.broadcasted_iota(jnp.int32, (1, KVW), 1)
    lane_v = lax.broadcasted_iota(jnp.int32, (1, 2 * NSA_HD), 1)
    ql = lax.broadcasted_iota(jnp.int32, (tq, 1), 0)
    klw = lax.broadcasted_iota(jnp.int32, (1, tq), 1)
    selm = selm_ref[0]
    o_sel = [[None] * NSA_GROUP for _ in range(NSA_KV)]
    o_win = [[None] * NSA_GROUP for _ in range(NSA_KV)]

    def reset():
        m_s[...] = jnp.full(m_s.shape, NEG_INF, F32)
        acc_s[...] = jnp.zeros(acc_s.shape, F32)

    for g in range(NSA_KV):
        own = _slot(lane, g)
        pick = _slot(lane, (g + 1) % NSA_KV)
        qg = jnp.concatenate(
            [jnp.where(own, q_ref[0, :, h * KVW:(h + 1) * KVW],
                       jnp.where(pick, selm, sf_ref[g * NSA_GROUP + h].astype(BF16)))
             for h in range(NSA_GROUP)], axis=0)

        reset()
        last = (q0 + tq + tks - 1) // tks - 1

        def body(kt, carry):
            k0 = pl.multiple_of(kt * tks, tks)
            _softmax_tile(qg, ks_ref[0, g, pl.ds(k0, tks), :], vs_ref[0, g, pl.ds(k0, tks), :], None,
                          tq, m_s, acc_s)
            return carry

        lax.fori_loop(0, last, body, 0)
        k0 = pl.multiple_of(last * tks, tks)
        kls = lax.broadcasted_iota(jnp.int32, (1, tks), 1)
        _softmax_tile(qg, ks_ref[0, g, pl.ds(k0, tks), :], vs_ref[0, g, pl.ds(k0, tks), :],
                      (kls + k0) <= (ql + q0), tq, m_s, acc_s)
        osg = _finish(g, acc_s)
        for h in range(NSA_GROUP):
            o_sel[g][h] = osg[h * tq:(h + 1) * tq]

        reset()
        for dj in range(nwt, 0, -1):
            mask = (klw > ql) if dj == nwt else None

            @pl.when(qi >= dj)
            def _(dj=dj, mask=mask):
                _softmax_tile(qg, kw_refs[nwt - dj][0, g], vw_refs[nwt - dj][0, g], mask, tq, m_s, acc_s)

        _softmax_tile(qg, kw_refs[nwt][0, g], vw_refs[nwt][0, g], klw <= ql, tq, m_s, acc_s)
        owg = _finish(g, acc_s)
        for h in range(NSA_GROUP):
            o_win[g][h] = owg[h * tq:(h + 1) * tq]

    def assemble(parts):
        first = lane_v < NSA_HD
        cols = []
        for h in range(NSA_GROUP):
            for pair in range(NSA_KV // 2):
                cols.append(jnp.where(first, parts[2 * pair][h], parts[2 * pair + 1][h]))
        return jnp.concatenate(cols, axis=1)

    gate = gate_ref[0]
    ghi = gate.astype(BF16)
    glo = (gate - ghi.astype(F32)).astype(BF16)
    branches = (oc_ref[0], assemble(o_sel), assemble(o_win))
    o = jnp.zeros(branches[0].shape, F32)
    for c in range(3):
        gc = _mm(ghi, gexp_ref[c]) + _mm(glo, gexp_ref[c])
        o = o + gc * branches[c]
    y = _mm(o.astype(BF16), wout_ref[...])
    o_ref[0] = x_ref[0] + g1_ref[0] * y


def _attend(q, ks, vs, kw, vw, selm, sf, oc, gate, gexp, x, g1, w_out):
    bsz, seq, d = x.shape
    G = NSA_KV
    tq = min(256, seq)
    tks = min(512, seq)
    nwt = WINDOW // tq
    tile = lambda w: pl.BlockSpec((1, tq, w), lambda b, i: (b, i, 0))
    kv = lambda w: pl.BlockSpec((1, G, seq, w), lambda b, i: (b, 0, 0, 0))
    wtile = lambda w, dj: pl.BlockSpec((1, G, tq, w), lambda b, i: (b, 0, jnp.maximum(i - dj, 0), 0))
    full = lambda a: pl.BlockSpec(a.shape, lambda b, i: tuple(0 for _ in a.shape))
    kw_specs = [wtile(KVW, dj) for dj in range(nwt, -1, -1)]
    vw_specs = [wtile(2 * NSA_HD, dj) for dj in range(nwt, -1, -1)]

    def body(q_ref, ks_ref, vs_ref, *rest):
        kw_refs = rest[:nwt + 1]
        vw_refs = rest[nwt + 1:2 * nwt + 2]
        _att_body(tks, q_ref, ks_ref, vs_ref, kw_refs, vw_refs, *rest[2 * nwt + 2:])

    return pl.pallas_call(
        body,
        grid=(bsz, seq // tq),
        in_specs=[tile(d), kv(KVW), kv(2 * NSA_HD), *kw_specs, *vw_specs, tile(KVW), full(sf), tile(d),
                  tile(128), full(gexp), tile(d), pl.BlockSpec((1, 1, d), lambda b, i: (b, 0, 0)),
                  full(w_out)],
        out_specs=tile(d),
        out_shape=jax.ShapeDtypeStruct(x.shape, F32),
        scratch_shapes=[pltpu.VMEM((NSA_GROUP * tq, 128), F32),
                        pltpu.VMEM((NSA_GROUP * tq, 2 * NSA_HD), F32)],
        compiler_params=_params(("arbitrary", "arbitrary")),
        name="nsa_attend",
    )(q, ks, vs, *([kw] * (nwt + 1)), *([vw] * (nwt + 1)), selm, sf, oc, gate, gexp, x, g1, w_out)


def _nsa_layer(x, nw, sc, sh, g1, w_in, w_out, cmp_pe, cmp_w1, cmp_w2):
    bsz, seq, d = x.shape
    G, HPG, HD = NSA_KV, NSA_GROUP, NSA_HD
    wq = w_in[:, :d].reshape(d, G, HPG, HD).transpose(0, 2, 1, 3).reshape(d, d)
    ngate = 3 * NSA_HEADS
    wgate = jnp.pad(w_in[:, d + 6 * KVW:], ((0, 0), (0, 128 - ngate)))
    w_cat = jnp.concatenate([wq, w_in[:, d:d + 6 * KVW], wgate], axis=1).astype(BF16)
    w_out_p = w_out.reshape(G, HPG, HD, d).transpose(1, 0, 2, 3).reshape(d, d).astype(BF16)

    ns = seq // SEL_LEN
    assert ns <= HD and CMP_LEN == 2 * CMP_STRIDE
    pos = np.arange(seq)
    fks = np.zeros((G, seq, KVW), np.float32)
    fkw = np.zeros((G, seq, KVW), np.float32)
    fv = np.zeros((G, 1, 2 * HD), np.float32)
    sf = np.zeros((G * HPG, 1, KVW), np.float32)
    slopes = _alibi_slopes()
    for g in range(G):
        pick0, feat0 = ((g + 1) % G) * HD, ((g + 2) % G) * HD
        fks[g, pos, pick0 + pos // SEL_LEN] = 1.0
        for arr in (fks, fkw):
            arr[g, :, feat0 + 0] = arr[g, :, feat0 + 2] = (pos // 64) * 64
            arr[g, :, feat0 + 1] = arr[g, :, feat0 + 3] = pos % 64
        fv[g, 0, ((g + 1) % 2) * HD] = 1.0
        for h in range(HPG):
            s32 = np.float32(slopes[g][h])
            hi = s32.astype(BF16).astype(np.float32)
            lo = np.float32(s32 - hi).astype(BF16).astype(np.float32)
            sf[g * HPG + h, 0, feat0:feat0 + 4] = (hi, hi, lo, lo)
    q, kvc, ks, vs, kw, vw, gate = _nsa_in(x, nw, sc, sh, w_cat, jnp.asarray(fks, BF16),
                                           jnp.asarray(fkw, BF16), jnp.asarray(fv))

    nseg = seq // CMP_STRIDE

    def strides(t):
        return t.reshape(bsz, nseg, CMP_STRIDE, G, HD).transpose(0, 3, 1, 2, 4).reshape(
            bsz, G, nseg, CMP_STRIDE * HD)

    pe = cmp_pe.reshape(2, 2, CMP_STRIDE * HD)
    w1 = cmp_w1.reshape(2, 2, CMP_STRIDE * HD, HD).astype(BF16)
    kc, vc = _compress(strides(kvc[..., :KVW]), strides(kvc[..., KVW:]), pe, w1, cmp_w2.astype(BF16))
    lanes = lambda t: t.transpose(0, 2, 1, 3).reshape(bsz, nseg, KVW).astype(BF16)

    ci = np.arange(nseg)[:, None] * CMP_STRIDE
    sj = np.arange(ns)[None, :] * SEL_LEN
    overlap = ((ci <= sj + SEL_LEN - 1) & (ci + CMP_LEN - 1 >= sj)
               & (np.arange(nseg)[:, None] < nseg - 1))
    ovt = jnp.asarray(overlap.T, F32)
    oc, selt = _select(q, lanes(kc), lanes(vc), ovt)
    selm = selt.transpose(0, 2, 1).astype(BF16)

    gexp = np.zeros((3, 128, d), np.float32)
    for g in range(G):
        for h in range(HPG):
            for c in range(3):
                gexp[c, g * HPG * 3 + h * 3 + c, (h * G + g) * HD:(h * G + g + 1) * HD] = 1.0
    return _attend(q, ks, vs, kw, vw, selm, jnp.asarray(sf), oc, gate, jnp.asarray(gexp, BF16), x, g1,
                   w_out_p)


def kernel(x, c, ada_w, ada_b, norm_mix, norm_ffn, final_norm, hg_w_in, hg_w_out, hg_gnorm, hg_lb,
           nsa_w_in, nsa_w_out, nsa_cmp_pe, nsa_cmp_w1, nsa_cmp_w2, ffn_w_up, ffn_conv_w, ffn_conv_b,
           ffn_w_down):
    depth = ada_w.shape[0]
    d = x.shape[-1]
    mod = _ada(c, ada_w, ada_b)
    for layer in range(depth):
        sh1, sc1, g1, sh2, sc2, g2 = [mod[layer, :, None, k * d:(k + 1) * d] for k in range(6)]
        j = layer // 2
        if layer % 2 == 0:
            x = _hgrn_layer(x, norm_mix[layer], sc1, sh1, g1, hg_w_in[j], hg_w_out[j], hg_gnorm[j], hg_lb, j)
        else:
            x = _nsa_layer(x, norm_mix[layer], sc1, sh1, g1, nsa_w_in[j], nsa_w_out[j], nsa_cmp_pe[j],
                           nsa_cmp_w1[j], nsa_cmp_w2[j])
        x = _ffn_layer(x, norm_ffn[layer], sc2, sh2, g2, ffn_w_up[layer], ffn_conv_w[layer],
                       ffn_conv_b[layer], ffn_w_down[layer], final_norm, layer == depth - 1)
    return x
```

```python
import functools

import numpy as np
import jax
import jax.numpy as jnp
from jax import lax
from jax.experimental import pallas as pl
from jax.experimental.pallas import tpu as pltpu

F32 = jnp.float32
BF16 = jnp.bfloat16
HIGHEST = lax.Precision.HIGHEST

EPS = 1e-6
NEG_INF = -1e30
LOG2_E = 1.4426950408889634
HG_HEADS = 8
HG_DK = 128
HG_CHUNK = 64
HG_SUB = 8
NSA_HEADS = 16
NSA_KV = 4
NSA_GROUP = NSA_HEADS // NSA_KV
NSA_HD = 64
CMP_LEN = 32
CMP_STRIDE = 16
SEL_LEN = 64
SEL_TOPK = 16
WINDOW = 512
FORCED_SCORE = 1e4
CONV_W = 3
KVW = NSA_KV * NSA_HD

VMEM_LIMIT_BYTES = 56 * 1024 * 1024
FFN_HALO = 16


def _mm(a, b):
    return jnp.dot(a, b, preferred_element_type=F32)


def _nt(a, b):
    return lax.dot_general(a, b, (((1,), (1,)), ((), ())), preferred_element_type=F32)


def _tn(a, b):
    return lax.dot_general(a, b, (((0,), (0,)), ((), ())), preferred_element_type=F32)


def _sigmoid(x):
    return 1.0 / (1.0 + jnp.exp(-x))


def _norm_mod(x, nw, sc, sh):
    ms = jnp.mean(x * x, axis=-1, keepdims=True)
    return (x * lax.rsqrt(ms + EPS) * nw) * (1.0 + sc) + sh


def _params(sem):
    return pltpu.CompilerParams(dimension_semantics=sem, vmem_limit_bytes=VMEM_LIMIT_BYTES)


def _ada_body(c_ref, w_ref, b_ref, o_ref):
    c = c_ref[...]
    ca = c * _sigmoid(c)
    o_ref[0] = jnp.dot(ca, w_ref[0], preferred_element_type=F32, precision=HIGHEST) + b_ref[0]


def _ada(c, ada_w, ada_b):
    depth, d, n6 = ada_w.shape
    bsz = c.shape[0]
    tn = n6 // 4
    return pl.pallas_call(
        _ada_body,
        grid=(depth, n6 // tn),
        in_specs=[pl.BlockSpec((bsz, d), lambda l, j: (0, 0)),
                  pl.BlockSpec((1, d, tn), lambda l, j: (l, 0, j)),
                  pl.BlockSpec((1, 1, tn), lambda l, j: (l, 0, j))],
        out_specs=pl.BlockSpec((1, bsz, tn), lambda l, j: (l, 0, j)),
        out_shape=jax.ShapeDtypeStruct((depth, bsz, n6), F32),
        compiler_params=_params(("arbitrary", "arbitrary")),
        name="ada_mod",
    )(c, ada_w, ada_b.reshape(depth, 1, n6))


def _pair_offset(j, i):
    nb = HG_CHUNK // HG_SUB
    before = sum(nb - 1 - jj for jj in range(j))
    return HG_SUB * (before + (i - j - 1))


def _hgrn_scale_stage(q_s, k_s, b2_s, b_s, r0, ls):
    C, SB = HG_CHUNK, HG_SUB
    nb = C // SB
    rows = pl.ds(r0, C)
    q = q_s[rows, ls]
    b = b2_s[rows, ls]
    b_s[0, :, ls] = b
    b_s[1, :, ls] = k_s[rows, ls]
    bq = [b[SB * i:SB * (i + 1)] for i in range(nb)]
    qq = [q[SB * i:SB * (i + 1)] for i in range(nb)]
    bend = [b_s[0, SB * j + SB - 1:SB * j + SB, ls] for j in range(nb)]
    lhs = []
    for j in range(nb - 1):
        for i in range(j + 1, nb):
            lhs.append(qq[i] * jnp.exp2(bq[i] - bend[j]))
    prod = []
    for i in range(nb):
        for s in range(SB):
            r = SB * i + s
            brow = b_s[0, r:r + 1, ls]
            krow = b_s[1, r:r + 1, ls]
            arg = bq[i] - brow
            if s > 0:
                arg = jnp.minimum(arg, 0.0)
            prod.append(qq[i] * krow * jnp.exp2(arg))
    return jnp.concatenate(lhs, axis=0).astype(BF16), jnp.concatenate(prod, axis=0).astype(BF16)


def _hgrn_score_stage(scaled, kt_s, r0, ls, ones_r, lane, lane_blk, causal):
    C, SB = HG_CHUNK, HG_SUB
    nb = C // SB
    lhs, prod = scaled
    res = _nt(lhs, kt_s[pl.ds(r0, C), ls])
    red = _mm(prod, ones_r)
    srows = []
    for i in range(nb):
        s_i = jnp.zeros((SB, C), F32)
        for j in range(i):
            off = _pair_offset(j, i)
            s_i = jnp.where(lane_blk == j, res[off:off + SB], s_i)
        for s in range(SB):
            r = SB * i + s
            s_i = jnp.where(lane == r, red[SB * r:SB * (r + 1)], s_i)
        srows.append(s_i)
    return jnp.where(causal, jnp.concatenate(srows, axis=0), 0.0).astype(BF16)


def _hgrn_output_stage(scores, v_s, qd_s, kd_s, dec_s, st_s, hidx, c, r0, ls):
    rows = pl.ds(r0, HG_CHUNK)
    v = v_s[rows, ls]
    st = st_s[hidx]
    o = _nt(qd_s[rows, ls], st.astype(BF16)) + _mm(scores, v)
    st_s[hidx] = st * dec_s[c, 0:1, ls] + _tn(v, kd_s[rows, ls])
    return o


def _hgrn_body(lb_row, x_ref, nw_ref, sc_ref, sh_ref, g1_ref, wq_ref, wf_ref, wi_ref, wg_ref, wout_ref,
               gn_ref, lb_ref, o_ref, h_s, q_s, k_s, b2_s, v_s, qd_s, kt_s, kd_s, dec_s, b_s, sc_s, oh_s, og_s,
               st_s):
    si = pl.program_id(1)
    hp = pl.program_id(2)
    ts = x_ref.shape[1]
    width = q_s.shape[1]
    hpb = width // HG_DK
    C, SB = HG_CHUNK, HG_SUB

    @pl.when(si == 0)
    def _():
        for hh in range(hpb):
            st_s[hp * hpb + hh] = jnp.zeros((HG_DK, HG_DK), F32)

    @pl.when(hp == 0)
    def _():
        h_s[...] = _norm_mod(x_ref[0], nw_ref[...], sc_ref[0], sh_ref[0]).astype(BF16)

    h = h_s[...]
    lbraw = lb_ref[...]
    e = jnp.exp(lbraw - jnp.max(lbraw, axis=0, keepdims=True))
    lb = jnp.sum(e[:lb_row + 1], axis=0, keepdims=True) / jnp.sum(e, axis=0, keepdims=True)
    q_s[...] = _mm(h, wq_ref[...])
    f = _mm(h, wf_ref[...])
    ef = jnp.exp(-jnp.abs(f))
    rcp = 1.0 / (1.0 + ef)
    pos = f >= 0
    sig_p = jnp.where(pos, rcp, ef * rcp)
    sig_n = jnp.where(pos, ef * rcp, rcp)
    lf2 = jnp.log(lb + (1.0 - lb) * sig_p) * LOG2_E
    k_s[...] = (1.0 - lb) * sig_n
    v_s[...] = _mm(h, wi_ref[...]).astype(BF16)

    ri = lax.broadcasted_iota(jnp.int32, (C, C), 0)
    ci = lax.broadcasted_iota(jnp.int32, (C, C), 1)
    causal = ri >= ci
    tril = jnp.where(causal, 1.0, 0.0).astype(BF16)
    hi = lf2.astype(BF16)
    r1 = lf2 - hi.astype(F32)
    mid = r1.astype(BF16)
    lo = (r1 - mid.astype(F32)).astype(BF16)
    pieces = jnp.concatenate([hi, mid, lo], axis=1)
    for c in range(ts // C):
        rows = slice(c * C, (c + 1) * C)
        cs = _mm(tril, pieces[rows])
        b = cs[:, :width] + cs[:, width:2 * width] + cs[:, 2 * width:]
        bend = jnp.concatenate([jnp.broadcast_to(b[j + SB - 1:j + SB], (SB, width)) for j in range(0, C, SB)],
                               axis=0)
        blast = jnp.broadcast_to(b[C - 1:C], (C, width))
        k = k_s[rows]
        b2_s[rows] = b
        qd_s[rows] = (q_s[rows] * jnp.exp2(b)).astype(BF16)
        kt_s[rows] = (k * jnp.exp2(bend - b)).astype(BF16)
        kd_s[rows] = (k * jnp.exp2(blast - b)).astype(BF16)
        dec_s[c] = jnp.exp2(blast[0:SB])

    ones_r = jnp.ones((HG_DK, C), BF16)
    lane = lax.broadcasted_iota(jnp.int32, (HG_SUB, C), 1)
    lane_blk = lane // HG_SUB

    lanes = lambda hh: slice(hh * HG_DK, (hh + 1) * HG_DK)

    def scores_of(c):
        r0 = c * C if isinstance(c, int) else pl.multiple_of(c * C, C)
        for hh in range(hpb):
            scaled = _hgrn_scale_stage(q_s, k_s, b2_s, b_s, r0, lanes(hh))
            sc_s[c % 2, hh] = _hgrn_score_stage(scaled, kt_s, r0, lanes(hh), ones_r, lane, lane_blk, causal)

    def outputs_of(c):
        r0 = c * C if isinstance(c, int) else pl.multiple_of(c * C, C)
        for hh in range(hpb):
            oh_s[pl.ds(r0, C), lanes(hh)] = _hgrn_output_stage(
                sc_s[c % 2, hh], v_s, qd_s, kd_s, dec_s, st_s, hp * hpb + hh, c, r0, lanes(hh))

    nchunk = ts // C
    scores_of(0)

    def chunk(c, carry):
        scores_of(c + 1)
        outputs_of(c)
        return carry

    lax.fori_loop(0, nchunk - 1, chunk, 0)
    outputs_of(nchunk - 1)

    g = _mm(h, wg_ref[...])
    gate = g * _sigmoid(g)
    gn = gn_ref[...]
    parts = []
    for hh in range(hpb):
        ls = slice(hh * HG_DK, (hh + 1) * HG_DK)
        oh = oh_s[:, ls]
        ms = jnp.mean(oh * oh, axis=-1, keepdims=True)
        parts.append(oh * lax.rsqrt(ms + EPS) * gn)
    og_s[hp] = (jnp.concatenate(parts, axis=1) * gate).astype(BF16)

    nblk = og_s.shape[0]

    @pl.when(hp == nblk - 1)
    def _():
        y = jnp.zeros(o_ref.shape[1:], F32)
        for blk in range(nblk):
            y = y + _mm(og_s[blk], wout_ref[blk * width:(blk + 1) * width, :])
        o_ref[0] = x_ref[0] + g1_ref[0] * y


def _hgrn_layer(x, nw, sc, sh, g1, w_in, w_out, gnorm, hg_lb, lb_row):
    bsz, seq, d = x.shape
    ts = min(512, seq)
    hpb = HG_HEADS
    width = hpb * HG_DK
    nblk = d // width
    w_in = w_in.astype(BF16)
    seg = lambda k: pl.BlockSpec((d, width), lambda b, s, hp, k=k: (0, k * nblk + hp))
    vec = pl.BlockSpec((1, 1, d), lambda b, s, hp: (b, 0, 0))
    return pl.pallas_call(
        functools.partial(_hgrn_body, lb_row),
        grid=(bsz, seq // ts, nblk),
        in_specs=[pl.BlockSpec((1, ts, d), lambda b, s, hp: (b, s, 0)),
                  pl.BlockSpec((1, d), lambda b, s, hp: (0, 0)),
                  vec, vec, vec,
                  seg(0), seg(1), seg(2), seg(3),
                  pl.BlockSpec((d, d), lambda b, s, hp: (0, 0)),
                  pl.BlockSpec((1, HG_DK), lambda b, s, hp: (0, 0)),
                  pl.BlockSpec((hg_lb.shape[0], width), lambda b, s, hp: (0, hp))],
        out_specs=pl.BlockSpec((1, ts, d), lambda b, s, hp: (b, s, 0)),
        out_shape=jax.ShapeDtypeStruct(x.shape, F32),
        scratch_shapes=[pltpu.VMEM((ts, d), BF16),
                        pltpu.VMEM((ts, width), F32), pltpu.VMEM((ts, width), F32),
                        pltpu.VMEM((ts, width), F32), pltpu.VMEM((ts, width), BF16),
                        pltpu.VMEM((ts, width), BF16), pltpu.VMEM((ts, width), BF16),
                        pltpu.VMEM((ts, width), BF16),
                        pltpu.VMEM((ts // HG_CHUNK, HG_SUB, width), F32),
                        pltpu.VMEM((2, HG_CHUNK, width), F32),
                        pltpu.VMEM((2, hpb, HG_CHUNK, HG_CHUNK), BF16),
                        pltpu.VMEM((ts, width), F32),
                        pltpu.VMEM((nblk, ts, width), BF16),
                        pltpu.VMEM((HG_HEADS, HG_DK, HG_DK), F32)],
        compiler_params=_params(("arbitrary", "arbitrary", "arbitrary")),
        name="hgrn2_layer",
    )(x, nw.reshape(1, d), sc, sh, g1, w_in, w_in, w_in, w_in, w_out.astype(BF16),
      gnorm.reshape(1, HG_DK), hg_lb)


def _ffn_body(final, nchunk, x_ref, xh_ref, nw_ref, sc_ref, sh_ref, g2_ref, wa_ref, wv_ref, cw_ref, cb_ref,
              wd_ref, fn_ref, o_ref, hx_s, a_s):
    i = pl.program_id(1)
    tm = x_ref.shape[1]
    H = FFN_HALO
    x = x_ref[0]
    nw, sc, sh = nw_ref[...], sc_ref[0], sh_ref[0]
    hx_s[H:, :] = _norm_mod(x, nw, sc, sh).astype(BF16)
    hx_s[0:H, :] = jnp.where(i > 0, _norm_mod(xh_ref[0], nw, sc, sh), 0.0).astype(BF16)
    ff = wa_ref.shape[1]
    fc = ff // nchunk
    y = jnp.zeros((tm, x.shape[1]), F32)
    for c in range(nchunk):
        cols = slice(c * fc, (c + 1) * fc)
        a_s[...] = _mm(hx_s[...], wa_ref[:, cols])
        v = _mm(hx_s[H:, :], wv_ref[:, cols])
        cw = cw_ref[:, cols]
        conv = (cw[0:1] * a_s[H - 2:H - 2 + tm, :] + cw[1:2] * a_s[H - 1:H - 1 + tm, :]
                + cw[2:3] * a_s[H:H + tm, :] + cb_ref[:, cols])
        u = (conv * _sigmoid(conv) * v).astype(BF16)
        y = y + _mm(u, wd_ref[cols, :])
    out = x + g2_ref[0] * y
    if final:
        ms = jnp.mean(out * out, axis=-1, keepdims=True)
        out = out * lax.rsqrt(ms + EPS) * fn_ref[...]
    o_ref[0] = out


def _ffn_layer(x, nw, sc, sh, g2, w_up, conv_w, conv_b, w_down, final_norm, final):
    bsz, seq, d = x.shape
    ff = w_down.shape[0]
    tm = min(512, seq)
    nchunk = 2
    H = FFN_HALO
    wa = w_up[:, :ff].astype(BF16)
    wv = w_up[:, ff:].astype(BF16)
    vec = pl.BlockSpec((1, 1, d), lambda b, i: (b, 0, 0))
    full = lambda shape: pl.BlockSpec(shape, lambda b, i: tuple(0 for _ in shape))
    return pl.pallas_call(
        functools.partial(_ffn_body, final, nchunk),
        grid=(bsz, seq // tm),
        in_specs=[pl.BlockSpec((1, tm, d), lambda b, i: (b, i, 0)),
                  pl.BlockSpec((1, H, d), lambda b, i: (b, jnp.maximum(i * (tm // H) - 1, 0), 0)),
                  full((1, d)), vec, vec, vec,
                  full((d, ff)), full((d, ff)), full((CONV_W, ff)), full((1, ff)), full((ff, d)),
                  full((1, d))],
        out_specs=pl.BlockSpec((1, tm, d), lambda b, i: (b, i, 0)),
        out_shape=jax.ShapeDtypeStruct(x.shape, F32),
        scratch_shapes=[pltpu.VMEM((tm + H, d), BF16), pltpu.VMEM((tm + H, ff // nchunk), F32)],
        compiler_params=_params(("arbitrary", "arbitrary")),
        name="conv_ffn_layer",
    )(x, x, nw.reshape(1, d), sc, sh, g2, wa, wv, conv_w, conv_b.reshape(1, ff), w_down.astype(BF16),
      final_norm.reshape(1, d))


def _alibi_slopes():
    return [[2.0 ** (-8.0 * (g * NSA_GROUP + h + 1) / NSA_HEADS) for h in range(NSA_GROUP)]
            for g in range(NSA_KV)]


def _slot(lane, g):
    return (lane >= g * NSA_HD) & (lane < (g + 1) * NSA_HD)


def _nsa_in_body(x_ref, nw_ref, sc_ref, sh_ref, w_ref, fks_ref, fkw_ref, fv_ref,
                 q_ref, kvc_ref, ks_ref, vs_ref, kw_ref, vw_ref, gate_ref):
    d = x_ref.shape[2]
    h = _norm_mod(x_ref[0], nw_ref[...], sc_ref[0], sh_ref[0]).astype(BF16)
    p = _mm(h, w_ref[...])
    q_ref[0] = (p[:, :d] * (NSA_HD ** -0.5)).astype(BF16)
    o = d
    kvc_ref[0] = p[:, o:o + 2 * KVW]
    o += 2 * KVW
    ksb, vsb, kwb, vwb = [p[:, o + i * KVW:o + (i + 1) * KVW].astype(BF16) for i in range(4)]
    o += 4 * KVW
    lane = lax.broadcasted_iota(jnp.int32, (1, KVW), 1)
    lane_v = lax.broadcasted_iota(jnp.int32, (1, 2 * NSA_HD), 1)
    for g in range(NSA_KV):
        own = _slot(lane, g)
        ks_ref[0, g] = jnp.where(own, ksb, fks_ref[g])
        kw_ref[0, g] = jnp.where(own, kwb, fkw_ref[g])
        half = slice((g // 2) * 2 * NSA_HD, (g // 2 + 1) * 2 * NSA_HD)
        own_v = _slot(lane_v, g % 2)
        fv = fv_ref[g].astype(BF16)
        vs_ref[0, g] = jnp.where(own_v, vsb[:, half], fv)
        vw_ref[0, g] = jnp.where(own_v, vwb[:, half], fv)
    gate_ref[0] = _sigmoid(p[:, o:])


def _nsa_in(x, nw, sc, sh, w_cat, fks, fkw, fv):
    bsz, seq, d = x.shape
    tm = min(512, seq)
    ncol = w_cat.shape[1]
    G = NSA_KV
    tile = lambda w: pl.BlockSpec((1, tm, w), lambda b, i: (b, i, 0))
    gtile = lambda w: pl.BlockSpec((1, G, tm, w), lambda b, i: (b, 0, i, 0))
    vec = pl.BlockSpec((1, 1, d), lambda b, i: (b, 0, 0))
    sds = lambda w, dt: jax.ShapeDtypeStruct((bsz, seq, w), dt)
    gsds = lambda w: jax.ShapeDtypeStruct((bsz, G, seq, w), BF16)
    feat = pl.BlockSpec((G, tm, KVW), lambda b, i: (0, i, 0))
    return pl.pallas_call(
        _nsa_in_body,
        grid=(bsz, seq // tm),
        in_specs=[tile(d), pl.BlockSpec((1, d), lambda b, i: (0, 0)), vec, vec,
                  pl.BlockSpec((d, ncol), lambda b, i: (0, 0)), feat, feat,
                  pl.BlockSpec((G, 1, 2 * NSA_HD), lambda b, i: (0, 0, 0))],
        out_specs=[tile(d), tile(2 * KVW), gtile(KVW), gtile(2 * NSA_HD), gtile(KVW), gtile(2 * NSA_HD),
                   tile(128)],
        out_shape=[sds(d, BF16), sds(2 * KVW, F32), gsds(KVW), gsds(2 * NSA_HD), gsds(KVW),
                   gsds(2 * NSA_HD), sds(128, F32)],
        compiler_params=_params(("arbitrary", "arbitrary")),
        name="nsa_in_proj",
    )(x, nw.reshape(1, d), sc, sh, w_cat, fks, fkw, fv)


def _cmp_body(kseg_ref, vseg_ref, pe_ref, w1_ref, w2_ref, kc_ref, vc_ref):
    nseg = kseg_ref.shape[2]
    row = lax.broadcasted_iota(jnp.int32, (nseg, NSA_HD), 0)
    for i, (src, dst) in enumerate(((kseg_ref, kc_ref), (vseg_ref, vc_ref))):
        seg = src[0, 0]
        a = _mm((seg + pe_ref[i, 0:1, :]).astype(BF16), w1_ref[i, 0])
        bm = _mm((seg + pe_ref[i, 1:2, :]).astype(BF16), w1_ref[i, 1])
        pre = a + pltpu.roll(bm, nseg - 1, 0)
        act = pre * _sigmoid(pre)
        out = _mm(act.astype(BF16), w2_ref[i])
        dst[0, 0] = jnp.where(row < nseg - 1, out, 0.0)


def _compress(kseg, vseg, pe, w1, w2):
    bsz, ng, nseg, width = kseg.shape
    blk = pl.BlockSpec((1, 1, nseg, width), lambda b, g: (b, g, 0, 0))
    oblk = pl.BlockSpec((1, 1, nseg, NSA_HD), lambda b, g: (b, g, 0, 0))
    full = lambda a: pl.BlockSpec(a.shape, lambda b, g: tuple(0 for _ in a.shape))
    sds = jax.ShapeDtypeStruct((bsz, ng, nseg, NSA_HD), F32)
    return pl.pallas_call(
        _cmp_body,
        grid=(bsz, ng),
        in_specs=[blk, blk, full(pe), full(w1), full(w2)],
        out_specs=[oblk, oblk],
        out_shape=[sds, sds],
        compiler_params=_params(("arbitrary", "arbitrary")),
        name="nsa_compress",
    )(kseg, vseg, pe, w1, w2)


def _assemble_heads(parts):
    tq = parts[0][0].shape[0]
    first = lax.broadcasted_iota(jnp.int32, (1, 2 * NSA_HD), 1) < NSA_HD
    cols = []
    for h in range(NSA_GROUP):
        for pair in range(NSA_KV // 2):
            cols.append(jnp.where(first, parts[2 * pair][h], parts[2 * pair + 1][h]))
    return jnp.concatenate(cols, axis=1)


def _sel_body(nsel, q_ref, kc_ref, vc_ref, sf_ref, ovt_ref, oc_ref, selt_ref):
    qi = pl.program_id(1)
    tq = q_ref.shape[1]
    n = kc_ref.shape[2]
    ns = ovt_ref.shape[0]
    q0 = qi * tq
    SB = 8
    tcol = q0 + lax.broadcasted_iota(jnp.int32, (tq, 1), 0)
    ncol = lax.broadcasted_iota(jnp.int32, (1, n), 1)
    valid = (ncol * CMP_STRIDE + (CMP_LEN - 1)) <= tcol
    rowok = jnp.where(tcol >= CMP_LEN - 1, 1.0, 0.0)
    lane = lax.broadcasted_iota(jnp.int32, (1, KVW), 1)
    jidx = lax.broadcasted_iota(jnp.int32, (ns, tq), 0)
    jloc = lax.broadcasted_iota(jnp.int32, (SB, tq), 0)
    tl = q0 + lax.broadcasted_iota(jnp.int32, (ns, tq), 1)
    cur = tl // SEL_LEN
    valid_s = jidx * SEL_LEN <= tl
    forced = ((jidx == 0) | (jidx == cur) | (jidx == cur - 1)) & valid_s
    parts = [[None] * NSA_GROUP for _ in range(NSA_KV)]
    for g in range(NSA_KV):
        own = _slot(lane, g)
        psum = jnp.zeros((tq, n), F32)
        ps = []
        for h in range(NSA_GROUP):
            qe = jnp.where(own, q_ref[0, :, h * KVW:(h + 1) * KVW], sf_ref[g * NSA_GROUP + h].astype(BF16))
            s = jnp.where(valid, _nt(qe, kc_ref[0, g]), NEG_INF)
            ex = jnp.exp(s - jnp.max(s, axis=-1, keepdims=True))
            p = ex * (rowok / jnp.sum(ex, axis=-1, keepdims=True))
            psum = psum + p
            ps.append(p.astype(BF16))
        og = _mm(jnp.concatenate(ps, axis=0), vc_ref[0, g])
        for h in range(NSA_GROUP):
            parts[g][h] = og[h * tq:(h + 1) * tq]
        imp = lax.dot_general(ovt_ref[...], psum, (((1,), (1,)), ((), ())),
                              preferred_element_type=F32, precision=HIGHEST)
        score = jnp.where(forced, FORCED_SCORE, jnp.where(valid_s, imp, -1.0))
        blocks = [score[SB * k:SB * (k + 1)] for k in range(ns // SB)]
        cnts = [jnp.zeros((SB, tq), F32) for _ in blocks]
        for jp in range(ns):
            row = score[jp:jp + 1, :]
            for k, blk in enumerate(blocks):
                if SB * k > jp:
                    beats = jnp.where(row >= blk, 1.0, 0.0)
                elif SB * (k + 1) <= jp:
                    beats = jnp.where(row > blk, 1.0, 0.0)
                else:
                    beats = jnp.where(jloc > jp - SB * k, jnp.where(row >= blk, 1.0, 0.0),
                                      jnp.where(row > blk, 1.0, 0.0))
                cnts[k] = cnts[k] + beats
        cnt = jnp.concatenate(cnts, axis=0)
        r0 = ((g + 1) % NSA_KV) * NSA_HD
        selt_ref[0, r0:r0 + ns, :] = jnp.where(cnt < nsel, 0.0, NEG_INF)
        if ns < NSA_HD:
            selt_ref[0, r0 + ns:r0 + NSA_HD, :] = jnp.zeros((NSA_HD - ns, tq), F32)
    oc_ref[0] = _assemble_heads(parts)


def _select(q, kc, vc, sf, ovt):
    bsz, seq, d = q.shape
    n = kc.shape[2]
    ns = ovt.shape[0]
    tq = min(256, seq)
    nsel = min(SEL_TOPK, ns)
    assert ns % 8 == 0
    return pl.pallas_call(
        functools.partial(_sel_body, nsel),
        grid=(bsz, seq // tq),
        in_specs=[pl.BlockSpec((1, tq, d), lambda b, i: (b, i, 0)),
                  pl.BlockSpec((1, NSA_KV, n, KVW), lambda b, i: (b, 0, 0, 0)),
                  pl.BlockSpec((1, NSA_KV, n, 2 * NSA_HD), lambda b, i: (b, 0, 0, 0)),
                  pl.BlockSpec(sf.shape, lambda b, i: (0, 0, 0)),
                  pl.BlockSpec((ns, n), lambda b, i: (0, 0))],
        out_specs=[pl.BlockSpec((1, tq, d), lambda b, i: (b, i, 0)),
                   pl.BlockSpec((1, KVW, tq), lambda b, i: (b, 0, i))],
        out_shape=[jax.ShapeDtypeStruct((bsz, seq, d), F32),
                   jax.ShapeDtypeStruct((bsz, KVW, seq), F32)],
        compiler_params=_params(("arbitrary", "arbitrary")),
        name="nsa_compressed_select",
    )(q, kc, vc, sf, ovt)


def _softmax_tile(s, vb, mask, tq, m_s, acc_s):
    tk = vb.shape[0]
    ps = []
    for h in range(NSA_GROUP):
        rows = slice(h * tq, (h + 1) * tq)
        sh = s[rows]
        if mask is not None:
            sh = jnp.where(mask, sh, NEG_INF)
        m_old = m_s[rows]
        m_new = jnp.maximum(m_old, jnp.max(sh, axis=-1, keepdims=True))
        p = jnp.exp(sh - jnp.concatenate([m_new] * (tk // 128), axis=1))
        m_s[rows] = m_new
        acc_s[rows] = acc_s[rows] * jnp.exp(m_old - m_new)
        ps.append(p.astype(BF16))
    acc_s[...] += _mm(jnp.concatenate(ps, axis=0), vb)


def _finish(g, acc_s):
    one = ((g + 1) % 2) * NSA_HD
    acc = acc_s[...]
    return acc * (1.0 / acc[:, one:one + 1])


def _att_body(tks, q_ref, ks_ref, vs_ref, kw_refs, vw_refs, selm_ref, sf_ref, oc_ref, gate_ref, gexp_ref,
              x_ref, g1_ref, wout_ref, o_ref, m_s, acc_s):
    qi = pl.program_id(1)
    tq = q_ref.shape[1]
    q0 = qi * tq
    nwt = len(kw_refs) - 1
    lane = lax.broadcasted_iota(jnp.int32, (1, KVW), 1)
    lane_v = lax.broadcasted_iota(jnp.int32, (1, 2 * NSA_HD), 1)
    ql = lax.broadcasted_iota(jnp.int32, (tq, 1), 0)
    colw = lax.broadcasted_iota(jnp.int32, (1, (nwt + 1) * tq), 1)
    rel = jnp.where(colw // tq + qi >= nwt, colw - nwt * tq - ql, 1)
    wmask = (rel <= 0) & (rel > -WINDOW)
    selm = selm_ref[0]
    o_sel = [[None] * NSA_GROUP for _ in range(NSA_KV)]
    o_win = [[None] * NSA_GROUP for _ in range(NSA_KV)]

    def reset():
        m_s[...] = jnp.full(m_s.shape, NEG_INF, F32)
        acc_s[...] = jnp.zeros(acc_s.shape, F32)

    for g in range(NSA_KV):
        own = _slot(lane, g)
        pick = _slot(lane, (g + 1) % NSA_KV)
        qg = jnp.concatenate(
            [jnp.where(own, q_ref[0, :, h * KVW:(h + 1) * KVW],
                       jnp.where(pick, selm, sf_ref[g * NSA_GROUP + h].astype(BF16)))
             for h in range(NSA_GROUP)], axis=0)

        reset()
        last = (q0 + tq + tks - 1) // tks - 1

        def tile(kt):
            return pl.ds(pl.multiple_of(kt * tks, tks), tks)

        def body(kt, carry):
            _softmax_tile(_nt(qg, ks_ref[0, g, tile(kt), :]), vs_ref[0, g, tile(kt), :], None, tq, m_s, acc_s)
            return carry

        lax.fori_loop(0, last, body, 0)
        kls = lax.broadcasted_iota(jnp.int32, (1, tks), 1)
        _softmax_tile(_nt(qg, ks_ref[0, g, tile(last), :]), vs_ref[0, g, tile(last), :],
                      (kls + last * tks) <= (ql + q0), tq, m_s, acc_s)
        osg = _finish(g, acc_s)
        for h in range(NSA_GROUP):
            o_sel[g][h] = osg[h * tq:(h + 1) * tq]

        kcat = jnp.concatenate([r[0, g] for r in kw_refs], axis=0)
        vcat = jnp.concatenate([r[0, g] for r in vw_refs], axis=0)
        s = _nt(qg, kcat)
        ps = []
        for h in range(NSA_GROUP):
            sh = jnp.where(wmask, s[h * tq:(h + 1) * tq], NEG_INF)
            ps.append(jnp.exp(sh - jnp.max(sh, axis=-1, keepdims=True)).astype(BF16))
        acc = _mm(jnp.concatenate(ps, axis=0), vcat)
        one = ((g + 1) % 2) * NSA_HD
        owg = acc * (1.0 / acc[:, one:one + 1])
        for h in range(NSA_GROUP):
            o_win[g][h] = owg[h * tq:(h + 1) * tq]

    assemble = _assemble_heads
    gate = gate_ref[0]
    ghi = gate.astype(BF16)
    glo = (gate - ghi.astype(F32)).astype(BF16)
    branches = (oc_ref[0], assemble(o_sel), assemble(o_win))
    o = jnp.zeros(branches[0].shape, F32)
    for c in range(3):
        gc = _mm(ghi, gexp_ref[c]) + _mm(glo, gexp_ref[c])
        o = o + gc * branches[c]
    y = _mm(o.astype(BF16), wout_ref[...])
    o_ref[0] = x_ref[0] + g1_ref[0] * y


def _attend(q, ks, vs, kw, vw, selm, sf, oc, gate, gexp, x, g1, w_out):
    bsz, seq, d = x.shape
    G = NSA_KV
    tq = min(256, seq)
    tks = min(512, seq)
    nwt = WINDOW // tq
    tile = lambda w: pl.BlockSpec((1, tq, w), lambda b, i: (b, i, 0))
    kv = lambda w: pl.BlockSpec((1, G, seq, w), lambda b, i: (b, 0, 0, 0))
    wtile = lambda w, dj: pl.BlockSpec((1, G, tq, w), lambda b, i: (b, 0, jnp.maximum(i - dj, 0), 0))
    full = lambda a: pl.BlockSpec(a.shape, lambda b, i: tuple(0 for _ in a.shape))
    kw_specs = [wtile(KVW, dj) for dj in range(nwt, -1, -1)]
    vw_specs = [wtile(2 * NSA_HD, dj) for dj in range(nwt, -1, -1)]

    def body(q_ref, ks_ref, vs_ref, *rest):
        kw_refs = rest[:nwt + 1]
        vw_refs = rest[nwt + 1:2 * nwt + 2]
        _att_body(tks, q_ref, ks_ref, vs_ref, kw_refs, vw_refs, *rest[2 * nwt + 2:])

    return pl.pallas_call(
        body,
        grid=(bsz, seq // tq),
        in_specs=[tile(d), kv(KVW), kv(2 * NSA_HD), *kw_specs, *vw_specs, tile(KVW), full(sf), tile(d),
                  tile(128), full(gexp), tile(d), pl.BlockSpec((1, 1, d), lambda b, i: (b, 0, 0)),
                  full(w_out)],
        out_specs=tile(d),
        out_shape=jax.ShapeDtypeStruct(x.shape, F32),
        scratch_shapes=[pltpu.VMEM((NSA_GROUP * tq, 128), F32),
                        pltpu.VMEM((NSA_GROUP * tq, 2 * NSA_HD), F32)],
        compiler_params=_params(("arbitrary", "arbitrary")),
        name="nsa_attend",
    )(q, ks, vs, *([kw] * (nwt + 1)), *([vw] * (nwt + 1)), selm, sf, oc, gate, gexp, x, g1, w_out)


def _nsa_layer(x, nw, sc, sh, g1, w_in, w_out, cmp_pe, cmp_w1, cmp_w2):
    bsz, seq, d = x.shape
    G, HPG, HD = NSA_KV, NSA_GROUP, NSA_HD
    wq = w_in[:, :d].reshape(d, G, HPG, HD).transpose(0, 2, 1, 3).reshape(d, d)
    ngate = 3 * NSA_HEADS
    wgate = jnp.pad(w_in[:, d + 6 * KVW:], ((0, 0), (0, 128 - ngate)))
    w_cat = jnp.concatenate([wq, w_in[:, d:d + 6 * KVW], wgate], axis=1).astype(BF16)
    w_out_p = w_out.reshape(G, HPG, HD, d).transpose(1, 0, 2, 3).reshape(d, d).astype(BF16)

    ns = seq // SEL_LEN
    assert ns <= HD and CMP_LEN == 2 * CMP_STRIDE
    pos = np.arange(seq)
    fks = np.zeros((G, seq, KVW), np.float32)
    fkw = np.zeros((G, seq, KVW), np.float32)
    fv = np.zeros((G, 1, 2 * HD), np.float32)
    sf = np.zeros((G * HPG, 1, KVW), np.float32)
    slopes = _alibi_slopes()
    for g in range(G):
        pick0, feat0 = ((g + 1) % G) * HD, ((g + 2) % G) * HD
        fks[g, pos, pick0 + pos // SEL_LEN] = 1.0
        for arr in (fks, fkw):
            arr[g, :, feat0 + 0] = arr[g, :, feat0 + 2] = (pos // 64) * 64
            arr[g, :, feat0 + 1] = arr[g, :, feat0 + 3] = pos % 64
        fv[g, 0, ((g + 1) % 2) * HD] = 1.0
        for h in range(HPG):
            s32 = np.float32(slopes[g][h])
            hi = s32.astype(BF16).astype(np.float32)
            lo = np.float32(s32 - hi).astype(BF16).astype(np.float32)
            sf[g * HPG + h, 0, feat0:feat0 + 4] = (hi, hi, lo, lo)
    q, kvc, ks, vs, kw, vw, gate = _nsa_in(x, nw, sc, sh, w_cat, jnp.asarray(fks, BF16),
                                           jnp.asarray(fkw, BF16), jnp.asarray(fv))

    nseg = seq // CMP_STRIDE

    def strides(t):
        return t.reshape(bsz, nseg, CMP_STRIDE, G, HD).transpose(0, 3, 1, 2, 4).reshape(
            bsz, G, nseg, CMP_STRIDE * HD)

    pe = cmp_pe.reshape(2, 2, CMP_STRIDE * HD)
    w1 = cmp_w1.reshape(2, 2, CMP_STRIDE * HD, HD).astype(BF16)
    kc, vc = _compress(strides(kvc[..., :KVW]), strides(kvc[..., KVW:]), pe, w1, cmp_w2.astype(BF16))
    mid = np.zeros((nseg, HD), np.float32)
    mid[:, 0] = mid[:, 2] = np.arange(nseg) * CMP_STRIDE
    mid[:, 1] = mid[:, 3] = (CMP_LEN - 1) / 2.0
    zero = jnp.zeros((bsz, nseg, HD), BF16)
    midb = jnp.broadcast_to(jnp.asarray(mid, BF16), (bsz, nseg, HD))
    kce, vce = [], []
    for g in range(G):
        slots = [zero] * G
        slots[g] = kc[:, g].astype(BF16)
        slots[(g + 2) % G] = midb
        kce.append(jnp.concatenate(slots, axis=-1))
        half = [zero, zero]
        half[g % 2] = vc[:, g].astype(BF16)
        vce.append(jnp.concatenate(half, axis=-1))
    kce = jnp.stack(kce, axis=1)
    vce = jnp.stack(vce, axis=1)

    ci = np.arange(nseg)[:, None] * CMP_STRIDE
    sj = np.arange(ns)[None, :] * SEL_LEN
    overlap = ((ci <= sj + SEL_LEN - 1) & (ci + CMP_LEN - 1 >= sj)
               & (np.arange(nseg)[:, None] < nseg - 1))
    ovt = jnp.asarray(overlap.T, F32)
    sf = jnp.asarray(sf)
    oc, selt = _select(q, kce, vce, sf, ovt)
    selm = selt.transpose(0, 2, 1).astype(BF16)

    gexp = np.zeros((3, 128, d), np.float32)
    for g in range(G):
        for h in range(HPG):
            for c in range(3):
                gexp[c, g * HPG * 3 + h * 3 + c, (h * G + g) * HD:(h * G + g + 1) * HD] = 1.0
    return _attend(q, ks, vs, kw, vw, selm, sf, oc, gate, jnp.asarray(gexp, BF16), x, g1,
                   w_out_p)


def kernel(x, c, ada_w, ada_b, norm_mix, norm_ffn, final_norm, hg_w_in, hg_w_out, hg_gnorm, hg_lb,
           nsa_w_in, nsa_w_out, nsa_cmp_pe, nsa_cmp_w1, nsa_cmp_w2, ffn_w_up, ffn_conv_w, ffn_conv_b,
           ffn_w_down):
    depth = ada_w.shape[0]
    d = x.shape[-1]
    mod = _ada(c, ada_w, ada_b)
    for layer in range(depth):
        sh1, sc1, g1, sh2, sc2, g2 = [mod[layer, :, None, k * d:(k + 1) * d] for k in range(6)]
        j = layer // 2
        if layer % 2 == 0:
            x = _hgrn_layer(x, norm_mix[layer], sc1, sh1, g1, hg_w_in[j], hg_w_out[j], hg_gnorm[j], hg_lb, j)
        else:
            x = _nsa_layer(x, norm_mix[layer], sc1, sh1, g1, nsa_w_in[j], nsa_w_out[j], nsa_cmp_pe[j],
                           nsa_cmp_w1[j], nsa_cmp_w2[j])
        x = _ffn_layer(x, norm_ffn[layer], sc2, sh2, g2, ffn_w_up[layer], ffn_conv_w[layer],
                       ffn_conv_b[layer], ffn_w_down[layer], final_norm, layer == depth - 1)
    return x
```

```python
import functools

import numpy as np
import jax
import jax.numpy as jnp
from jax import lax
from jax.experimental import pallas as pl
from jax.experimental.pallas import tpu as pltpu

F32 = jnp.float32
BF16 = jnp.bfloat16
HIGHEST = lax.Precision.HIGHEST

EPS = 1e-6
NEG_INF = -1e30
LOG2_E = 1.4426950408889634
HG_HEADS = 8
HG_DK = 128
HG_CHUNK = 64
HG_SUB = 8
NSA_HEADS = 16
NSA_KV = 4
NSA_GROUP = NSA_HEADS // NSA_KV
NSA_HD = 64
CMP_LEN = 32
CMP_STRIDE = 16
SEL_LEN = 64
SEL_TOPK = 16
WINDOW = 512
FORCED_SCORE = 1e4
CONV_W = 3
KVW = NSA_KV * NSA_HD

VMEM_LIMIT_BYTES = 56 * 1024 * 1024
FFN_HALO = 16


def _mm(a, b):
    return jnp.dot(a, b, preferred_element_type=F32)


def _nt(a, b):
    return lax.dot_general(a, b, (((1,), (1,)), ((), ())), preferred_element_type=F32)


def _tn(a, b):
    return lax.dot_general(a, b, (((0,), (0,)), ((), ())), preferred_element_type=F32)


def _sigmoid(x):
    return 1.0 / (1.0 + jnp.exp(-x))


def _norm_mod(x, nw, sc, sh):
    ms = jnp.mean(x * x, axis=-1, keepdims=True)
    return (x * lax.rsqrt(ms + EPS) * nw) * (1.0 + sc) + sh


def _params(sem):
    return pltpu.CompilerParams(dimension_semantics=sem, vmem_limit_bytes=VMEM_LIMIT_BYTES)


def _ada_body(c_ref, w_ref, b_ref, o_ref):
    c = c_ref[...]
    ca = c * _sigmoid(c)
    o_ref[0] = jnp.dot(ca, w_ref[0], preferred_element_type=F32, precision=HIGHEST) + b_ref[0]


def _ada(c, ada_w, ada_b):
    depth, d, n6 = ada_w.shape
    bsz = c.shape[0]
    tn = n6 // 4
    return pl.pallas_call(
        _ada_body,
        grid=(depth, n6 // tn),
        in_specs=[pl.BlockSpec((bsz, d), lambda l, j: (0, 0)),
                  pl.BlockSpec((1, d, tn), lambda l, j: (l, 0, j)),
                  pl.BlockSpec((1, 1, tn), lambda l, j: (l, 0, j))],
        out_specs=pl.BlockSpec((1, bsz, tn), lambda l, j: (l, 0, j)),
        out_shape=jax.ShapeDtypeStruct((depth, bsz, n6), F32),
        compiler_params=_params(("arbitrary", "arbitrary")),
        name="ada_mod",
    )(c, ada_w, ada_b.reshape(depth, 1, n6))


def _pair_offset(j, i):
    nb = HG_CHUNK // HG_SUB
    before = sum(nb - 1 - jj for jj in range(j))
    return HG_SUB * (before + (i - j - 1))


def _hgrn_scale_stage(q_s, k_s, b2_s, b_s, r0, ls):
    C, SB = HG_CHUNK, HG_SUB
    nb = C // SB
    rows = pl.ds(r0, C)
    q = q_s[rows, ls]
    b = b2_s[rows, ls]
    b_s[0, :, ls] = b
    b_s[1, :, ls] = k_s[rows, ls]
    bq = [b[SB * i:SB * (i + 1)] for i in range(nb)]
    qq = [q[SB * i:SB * (i + 1)] for i in range(nb)]
    bend = [b_s[0, SB * j + SB - 1:SB * j + SB, ls] for j in range(nb)]
    lhs = []
    for j in range(nb - 1):
        for i in range(j + 1, nb):
            lhs.append(qq[i] * jnp.exp2(bq[i] - bend[j]))
    prod = []
    for i in range(nb):
        for s in range(SB):
            r = SB * i + s
            brow = b_s[0, r:r + 1, ls]
            krow = b_s[1, r:r + 1, ls]
            arg = bq[i] - brow
            if s > 0:
                arg = jnp.minimum(arg, 0.0)
            prod.append(qq[i] * krow * jnp.exp2(arg))
    return jnp.concatenate(lhs, axis=0).astype(BF16), jnp.concatenate(prod, axis=0).astype(BF16)


def _hgrn_score_stage(scaled, kt_s, r0, ls, ones_r, lane, lane_blk, causal):
    C, SB = HG_CHUNK, HG_SUB
    nb = C // SB
    lhs, prod = scaled
    res = _nt(lhs, kt_s[pl.ds(r0, C), ls])
    red = _mm(prod, ones_r)
    srows = []
    for i in range(nb):
        s_i = jnp.zeros((SB, C), F32)
        for j in range(i):
            off = _pair_offset(j, i)
            s_i = jnp.where(lane_blk == j, res[off:off + SB], s_i)
        for s in range(SB):
            r = SB * i + s
            s_i = jnp.where(lane == r, red[SB * r:SB * (r + 1)], s_i)
        srows.append(s_i)
    return jnp.where(causal, jnp.concatenate(srows, axis=0), 0.0).astype(BF16)


def _hgrn_output_stage(scores, v_s, qd_s, kd_s, dec_s, st_s, hidx, c, r0, ls):
    rows = pl.ds(r0, HG_CHUNK)
    v = v_s[rows, ls]
    st = st_s[hidx]
    o = _nt(qd_s[rows, ls], st.astype(BF16)) + _mm(scores, v)
    st_s[hidx] = st * dec_s[c, 0:1, ls] + _tn(v, kd_s[rows, ls])
    return o


def _hgrn_body(lb_row, x_ref, nw_ref, sc_ref, sh_ref, g1_ref, wq_ref, wf_ref, wi_ref, wg_ref, wout_ref,
               gn_ref, lb_ref, o_ref, h_s, q_s, k_s, b2_s, v_s, qd_s, kt_s, kd_s, dec_s, b_s, sc_s, oh_s, og_s,
               st_s):
    si = pl.program_id(1)
    hp = pl.program_id(2)
    ts = x_ref.shape[1]
    width = q_s.shape[1]
    hpb = width // HG_DK
    C, SB = HG_CHUNK, HG_SUB

    @pl.when(si == 0)
    def _():
        for hh in range(hpb):
            st_s[hp * hpb + hh] = jnp.zeros((HG_DK, HG_DK), F32)

    @pl.when(hp == 0)
    def _():
        h_s[...] = _norm_mod(x_ref[0], nw_ref[...], sc_ref[0], sh_ref[0]).astype(BF16)

    h = h_s[...]
    lbraw = lb_ref[...]
    e = jnp.exp(lbraw - jnp.max(lbraw, axis=0, keepdims=True))
    lb = jnp.sum(e[:lb_row + 1], axis=0, keepdims=True) / jnp.sum(e, axis=0, keepdims=True)
    q_s[...] = _mm(h, wq_ref[...])
    f = _mm(h, wf_ref[...])
    ef = jnp.exp(-jnp.abs(f))
    rcp = 1.0 / (1.0 + ef)
    pos = f >= 0
    sig_p = jnp.where(pos, rcp, ef * rcp)
    sig_n = jnp.where(pos, ef * rcp, rcp)
    lf2 = jnp.log(lb + (1.0 - lb) * sig_p) * LOG2_E
    k_s[...] = (1.0 - lb) * sig_n
    v_s[...] = _mm(h, wi_ref[...]).astype(BF16)

    ri = lax.broadcasted_iota(jnp.int32, (C, C), 0)
    ci = lax.broadcasted_iota(jnp.int32, (C, C), 1)
    causal = ri >= ci
    tril = jnp.where(causal, 1.0, 0.0).astype(BF16)
    hi = lf2.astype(BF16)
    r1 = lf2 - hi.astype(F32)
    mid = r1.astype(BF16)
    lo = (r1 - mid.astype(F32)).astype(BF16)
    pieces = jnp.concatenate([hi, mid, lo], axis=1)
    for c in range(ts // C):
        rows = slice(c * C, (c + 1) * C)
        cs = _mm(tril, pieces[rows])
        b = cs[:, :width] + cs[:, width:2 * width] + cs[:, 2 * width:]
        bend = jnp.concatenate([jnp.broadcast_to(b[j + SB - 1:j + SB], (SB, width)) for j in range(0, C, SB)],
                               axis=0)
        blast = jnp.broadcast_to(b[C - 1:C], (C, width))
        k = k_s[rows]
        b2_s[rows] = b
        qd_s[rows] = (q_s[rows] * jnp.exp2(b)).astype(BF16)
        kt_s[rows] = (k * jnp.exp2(bend - b)).astype(BF16)
        kd_s[rows] = (k * jnp.exp2(blast - b)).astype(BF16)
        dec_s[c] = jnp.exp2(blast[0:SB])

    ones_r = jnp.ones((HG_DK, C), BF16)
    lane = lax.broadcasted_iota(jnp.int32, (HG_SUB, C), 1)
    lane_blk = lane // HG_SUB

    lanes = lambda hh: slice(hh * HG_DK, (hh + 1) * HG_DK)

    def scores_of(c):
        r0 = c * C if isinstance(c, int) else pl.multiple_of(c * C, C)
        for hh in range(hpb):
            scaled = _hgrn_scale_stage(q_s, k_s, b2_s, b_s, r0, lanes(hh))
            sc_s[c % 2, hh] = _hgrn_score_stage(scaled, kt_s, r0, lanes(hh), ones_r, lane, lane_blk, causal)

    def outputs_of(c):
        r0 = c * C if isinstance(c, int) else pl.multiple_of(c * C, C)
        for hh in range(hpb):
            oh_s[pl.ds(r0, C), lanes(hh)] = _hgrn_output_stage(
                sc_s[c % 2, hh], v_s, qd_s, kd_s, dec_s, st_s, hp * hpb + hh, c, r0, lanes(hh))

    nchunk = ts // C
    scores_of(0)

    def chunk(c, carry):
        scores_of(c + 1)
        outputs_of(c)
        return carry

    lax.fori_loop(0, nchunk - 1, chunk, 0)
    outputs_of(nchunk - 1)

    g = _mm(h, wg_ref[...])
    gate = g * _sigmoid(g)
    gn = gn_ref[...]
    parts = []
    for hh in range(hpb):
        ls = slice(hh * HG_DK, (hh + 1) * HG_DK)
        oh = oh_s[:, ls]
        ms = jnp.mean(oh * oh, axis=-1, keepdims=True)
        parts.append(oh * lax.rsqrt(ms + EPS) * gn)
    og_s[hp] = (jnp.concatenate(parts, axis=1) * gate).astype(BF16)

    nblk = og_s.shape[0]

    @pl.when(hp == nblk - 1)
    def _():
        y = jnp.zeros(o_ref.shape[1:], F32)
        for blk in range(nblk):
            y = y + _mm(og_s[blk], wout_ref[blk * width:(blk + 1) * width, :])
        o_ref[0] = x_ref[0] + g1_ref[0] * y


def _hgrn_layer(x, nw, sc, sh, g1, w_in, w_out, gnorm, hg_lb, lb_row):
    bsz, seq, d = x.shape
    ts = min(512, seq)
    hpb = HG_HEADS
    width = hpb * HG_DK
    nblk = d // width
    w_in = w_in.astype(BF16)
    seg = lambda k: pl.BlockSpec((d, width), lambda b, s, hp, k=k: (0, k * nblk + hp))
    vec = pl.BlockSpec((1, 1, d), lambda b, s, hp: (b, 0, 0))
    return pl.pallas_call(
        functools.partial(_hgrn_body, lb_row),
        grid=(bsz, seq // ts, nblk),
        in_specs=[pl.BlockSpec((1, ts, d), lambda b, s, hp: (b, s, 0)),
                  pl.BlockSpec((1, d), lambda b, s, hp: (0, 0)),
                  vec, vec, vec,
                  seg(0), seg(1), seg(2), seg(3),
                  pl.BlockSpec((d, d), lambda b, s, hp: (0, 0)),
                  pl.BlockSpec((1, HG_DK), lambda b, s, hp: (0, 0)),
                  pl.BlockSpec((hg_lb.shape[0], width), lambda b, s, hp: (0, hp))],
        out_specs=pl.BlockSpec((1, ts, d), lambda b, s, hp: (b, s, 0)),
        out_shape=jax.ShapeDtypeStruct(x.shape, F32),
        scratch_shapes=[pltpu.VMEM((ts, d), BF16),
                        pltpu.VMEM((ts, width), F32), pltpu.VMEM((ts, width), F32),
                        pltpu.VMEM((ts, width), F32), pltpu.VMEM((ts, width), BF16),
                        pltpu.VMEM((ts, width), BF16), pltpu.VMEM((ts, width), BF16),
                        pltpu.VMEM((ts, width), BF16),
                        pltpu.VMEM((ts // HG_CHUNK, HG_SUB, width), F32),
                        pltpu.VMEM((2, HG_CHUNK, width), F32),
                        pltpu.VMEM((2, hpb, HG_CHUNK, HG_CHUNK), BF16),
                        pltpu.VMEM((ts, width), F32),
                        pltpu.VMEM((nblk, ts, width), BF16),
                        pltpu.VMEM((HG_HEADS, HG_DK, HG_DK), F32)],
        compiler_params=_params(("arbitrary", "arbitrary", "arbitrary")),
        name="hgrn2_layer",
    )(x, nw.reshape(1, d), sc, sh, g1, w_in, w_in, w_in, w_in, w_out.astype(BF16),
      gnorm.reshape(1, HG_DK), hg_lb)


def _ffn_body(final, nchunk, x_ref, xh_ref, nw_ref, sc_ref, sh_ref, g2_ref, wa_ref, wv_ref, cw_ref, cb_ref,
              wd_ref, fn_ref, o_ref, hx_s, a_s):
    i = pl.program_id(1)
    tm = x_ref.shape[1]
    H = FFN_HALO
    x = x_ref[0]
    nw, sc, sh = nw_ref[...], sc_ref[0], sh_ref[0]
    hx_s[H:, :] = _norm_mod(x, nw, sc, sh).astype(BF16)
    hx_s[0:H, :] = jnp.where(i > 0, _norm_mod(xh_ref[0], nw, sc, sh), 0.0).astype(BF16)
    ff = wa_ref.shape[1]
    fc = ff // nchunk
    y = jnp.zeros((tm, x.shape[1]), F32)
    for c in range(nchunk):
        cols = slice(c * fc, (c + 1) * fc)
        a_s[...] = _mm(hx_s[...], wa_ref[:, cols])
        v = _mm(hx_s[H:, :], wv_ref[:, cols])
        cw = cw_ref[:, cols]
        conv = (cw[0:1] * a_s[H - 2:H - 2 + tm, :] + cw[1:2] * a_s[H - 1:H - 1 + tm, :]
                + cw[2:3] * a_s[H:H + tm, :] + cb_ref[:, cols])
        u = (conv * _sigmoid(conv) * v).astype(BF16)
        y = y + _mm(u, wd_ref[cols, :])
    out = x + g2_ref[0] * y
    if final:
        ms = jnp.mean(out * out, axis=-1, keepdims=True)
        out = out * lax.rsqrt(ms + EPS) * fn_ref[...]
    o_ref[0] = out


def _ffn_layer(x, nw, sc, sh, g2, w_up, conv_w, conv_b, w_down, final_norm, final):
    bsz, seq, d = x.shape
    ff = w_down.shape[0]
    tm = min(512, seq)
    nchunk = 2
    H = FFN_HALO
    wa = w_up[:, :ff].astype(BF16)
    wv = w_up[:, ff:].astype(BF16)
    vec = pl.BlockSpec((1, 1, d), lambda b, i: (b, 0, 0))
    full = lambda shape: pl.BlockSpec(shape, lambda b, i: tuple(0 for _ in shape))
    return pl.pallas_call(
        functools.partial(_ffn_body, final, nchunk),
        grid=(bsz, seq // tm),
        in_specs=[pl.BlockSpec((1, tm, d), lambda b, i: (b, i, 0)),
                  pl.BlockSpec((1, H, d), lambda b, i: (b, jnp.maximum(i * (tm // H) - 1, 0), 0)),
                  full((1, d)), vec, vec, vec,
                  full((d, ff)), full((d, ff)), full((CONV_W, ff)), full((1, ff)), full((ff, d)),
                  full((1, d))],
        out_specs=pl.BlockSpec((1, tm, d), lambda b, i: (b, i, 0)),
        out_shape=jax.ShapeDtypeStruct(x.shape, F32),
        scratch_shapes=[pltpu.VMEM((tm + H, d), BF16), pltpu.VMEM((tm + H, ff // nchunk), F32)],
        compiler_params=_params(("arbitrary", "arbitrary")),
        name="conv_ffn_layer",
    )(x, x, nw.reshape(1, d), sc, sh, g2, wa, wv, conv_w, conv_b.reshape(1, ff), w_down.astype(BF16),
      final_norm.reshape(1, d))


def _alibi_slopes():
    return [[2.0 ** (-8.0 * (g * NSA_GROUP + h + 1) / NSA_HEADS) for h in range(NSA_GROUP)]
            for g in range(NSA_KV)]


def _slot(lane, g):
    return (lane >= g * NSA_HD) & (lane < (g + 1) * NSA_HD)


def _nsa_in_body(x_ref, nw_ref, sc_ref, sh_ref, w_ref, fks_ref, fkw_ref, fv_ref,
                 q_ref, kvc_ref, ks_ref, vs_ref, kw_ref, vw_ref, gate_ref):
    d = x_ref.shape[2]
    h = _norm_mod(x_ref[0], nw_ref[...], sc_ref[0], sh_ref[0]).astype(BF16)
    p = _mm(h, w_ref[...])
    q_ref[0] = (p[:, :d] * (NSA_HD ** -0.5)).astype(BF16)
    o = d
    kvc_ref[0] = p[:, o:o + 2 * KVW]
    o += 2 * KVW
    ksb, vsb, kwb, vwb = [p[:, o + i * KVW:o + (i + 1) * KVW].astype(BF16) for i in range(4)]
    o += 4 * KVW
    lane = lax.broadcasted_iota(jnp.int32, (1, KVW), 1)
    lane_v = lax.broadcasted_iota(jnp.int32, (1, 2 * NSA_HD), 1)
    for g in range(NSA_KV):
        own = _slot(lane, g)
        ks_ref[0, g] = jnp.where(own, ksb, fks_ref[g])
        kw_ref[0, g] = jnp.where(own, kwb, fkw_ref[g])
        half = slice((g // 2) * 2 * NSA_HD, (g // 2 + 1) * 2 * NSA_HD)
        own_v = _slot(lane_v, g % 2)
        fv = fv_ref[g].astype(BF16)
        vs_ref[0, g] = jnp.where(own_v, vsb[:, half], fv)
        vw_ref[0, g] = jnp.where(own_v, vwb[:, half], fv)
    gate_ref[0] = _sigmoid(p[:, o:])


def _nsa_in(x, nw, sc, sh, w_cat, fks, fkw, fv):
    bsz, seq, d = x.shape
    tm = min(512, seq)
    ncol = w_cat.shape[1]
    G = NSA_KV
    tile = lambda w: pl.BlockSpec((1, tm, w), lambda b, i: (b, i, 0))
    gtile = lambda w: pl.BlockSpec((1, G, tm, w), lambda b, i: (b, 0, i, 0))
    vec = pl.BlockSpec((1, 1, d), lambda b, i: (b, 0, 0))
    sds = lambda w, dt: jax.ShapeDtypeStruct((bsz, seq, w), dt)
    gsds = lambda w: jax.ShapeDtypeStruct((bsz, G, seq, w), BF16)
    feat = pl.BlockSpec((G, tm, KVW), lambda b, i: (0, i, 0))
    return pl.pallas_call(
        _nsa_in_body,
        grid=(bsz, seq // tm),
        in_specs=[tile(d), pl.BlockSpec((1, d), lambda b, i: (0, 0)), vec, vec,
                  pl.BlockSpec((d, ncol), lambda b, i: (0, 0)), feat, feat,
                  pl.BlockSpec((G, 1, 2 * NSA_HD), lambda b, i: (0, 0, 0))],
        out_specs=[tile(d), tile(2 * KVW), gtile(KVW), gtile(2 * NSA_HD), gtile(KVW), gtile(2 * NSA_HD),
                   tile(128)],
        out_shape=[sds(d, BF16), sds(2 * KVW, F32), gsds(KVW), gsds(2 * NSA_HD), gsds(KVW),
                   gsds(2 * NSA_HD), sds(128, F32)],
        compiler_params=_params(("arbitrary", "arbitrary")),
        name="nsa_in_proj",
    )(x, nw.reshape(1, d), sc, sh, w_cat, fks, fkw, fv)


def _cmp_body(kseg_ref, vseg_ref, pe_ref, w1_ref, w2_ref, kc_ref, vc_ref):
    nseg = kseg_ref.shape[2]
    row = lax.broadcasted_iota(jnp.int32, (nseg, NSA_HD), 0)
    for i, (src, dst) in enumerate(((kseg_ref, kc_ref), (vseg_ref, vc_ref))):
        seg = src[0, 0]
        a = _mm((seg + pe_ref[i, 0:1, :]).astype(BF16), w1_ref[i, 0])
        bm = _mm((seg + pe_ref[i, 1:2, :]).astype(BF16), w1_ref[i, 1])
        pre = a + pltpu.roll(bm, nseg - 1, 0)
        act = pre * _sigmoid(pre)
        out = _mm(act.astype(BF16), w2_ref[i])
        dst[0, 0] = jnp.where(row < nseg - 1, out, 0.0)


def _compress(kseg, vseg, pe, w1, w2):
    bsz, ng, nseg, width = kseg.shape
    blk = pl.BlockSpec((1, 1, nseg, width), lambda b, g: (b, g, 0, 0))
    oblk = pl.BlockSpec((1, 1, nseg, NSA_HD), lambda b, g: (b, g, 0, 0))
    full = lambda a: pl.BlockSpec(a.shape, lambda b, g: tuple(0 for _ in a.shape))
    sds = jax.ShapeDtypeStruct((bsz, ng, nseg, NSA_HD), F32)
    return pl.pallas_call(
        _cmp_body,
        grid=(bsz, ng),
        in_specs=[blk, blk, full(pe), full(w1), full(w2)],
        out_specs=[oblk, oblk],
        out_shape=[sds, sds],
        compiler_params=_params(("arbitrary", "arbitrary")),
        name="nsa_compress",
    )(kseg, vseg, pe, w1, w2)


def _assemble_heads(parts):
    tq = parts[0][0].shape[0]
    first = lax.broadcasted_iota(jnp.int32, (1, 2 * NSA_HD), 1) < NSA_HD
    cols = []
    for h in range(NSA_GROUP):
        for pair in range(NSA_KV // 2):
            cols.append(jnp.where(first, parts[2 * pair][h], parts[2 * pair + 1][h]))
    return jnp.concatenate(cols, axis=1)


def _sel_body(nsel, q_ref, kc_ref, vc_ref, sf_ref, ovt_ref, oc_ref, selt_ref, flag_ref):
    qi = pl.program_id(1)
    tq = q_ref.shape[1]
    n = kc_ref.shape[2]
    ns = ovt_ref.shape[0]
    q0 = qi * tq
    SB = 8
    tcol = q0 + lax.broadcasted_iota(jnp.int32, (tq, 1), 0)
    ncol = lax.broadcasted_iota(jnp.int32, (1, n), 1)
    valid = (ncol * CMP_STRIDE + (CMP_LEN - 1)) <= tcol
    rowok = jnp.where(tcol >= CMP_LEN - 1, 1.0, 0.0)
    lane = lax.broadcasted_iota(jnp.int32, (1, KVW), 1)
    jidx = lax.broadcasted_iota(jnp.int32, (ns, tq), 0)
    jloc = lax.broadcasted_iota(jnp.int32, (SB, tq), 0)
    tl = q0 + lax.broadcasted_iota(jnp.int32, (ns, tq), 1)
    cur = tl // SEL_LEN
    valid_s = jidx * SEL_LEN <= tl
    forced = ((jidx == 0) | (jidx == cur) | (jidx == cur - 1)) & valid_s
    parts = [[None] * NSA_GROUP for _ in range(NSA_KV)]
    for g in range(NSA_KV):
        own = _slot(lane, g)
        psum = jnp.zeros((tq, n), F32)
        ps = []
        for h in range(NSA_GROUP):
            qe = jnp.where(own, q_ref[0, :, h * KVW:(h + 1) * KVW], sf_ref[g * NSA_GROUP + h].astype(BF16))
            s = jnp.where(valid, _nt(qe, kc_ref[0, g]), NEG_INF)
            ex = jnp.exp(s - jnp.max(s, axis=-1, keepdims=True))
            p = ex * (rowok / jnp.sum(ex, axis=-1, keepdims=True))
            psum = psum + p
            ps.append(p.astype(BF16))
        og = _mm(jnp.concatenate(ps, axis=0), vc_ref[0, g])
        for h in range(NSA_GROUP):
            parts[g][h] = og[h * tq:(h + 1) * tq]
        imp = lax.dot_general(ovt_ref[...], psum, (((1,), (1,)), ((), ())),
                              preferred_element_type=F32, precision=HIGHEST)
        score = jnp.where(forced, FORCED_SCORE, jnp.where(valid_s, imp, -1.0))
        blocks = [score[SB * k:SB * (k + 1)] for k in range(ns // SB)]
        cnts = [jnp.zeros((SB, tq), F32) for _ in blocks]
        for jp in range(ns):
            row = score[jp:jp + 1, :]
            for k, blk in enumerate(blocks):
                if SB * k > jp:
                    beats = jnp.where(row >= blk, 1.0, 0.0)
                elif SB * (k + 1) <= jp:
                    beats = jnp.where(row > blk, 1.0, 0.0)
                else:
                    beats = jnp.where(jloc > jp - SB * k, jnp.where(row >= blk, 1.0, 0.0),
                                      jnp.where(row > blk, 1.0, 0.0))
                cnts[k] = cnts[k] + beats
        cnt = jnp.concatenate(cnts, axis=0)
        r0 = ((g + 1) % NSA_KV) * NSA_HD
        selt_ref[0, r0:r0 + ns, :] = jnp.where(cnt < nsel, 0.0, NEG_INF)
        if ns < NSA_HD:
            selt_ref[0, r0 + ns:r0 + NSA_HD, :] = jnp.zeros((NSA_HD - ns, tq), F32)
        anyq = jnp.max(jnp.where(cnt < nsel, 1.0, 0.0), axis=1, keepdims=True)
        bpt = ns // flag_ref.shape[3]
        flag_ref[0, 0, g] = jnp.concatenate(
            [jnp.broadcast_to(jnp.max(anyq[bpt * kt:bpt * (kt + 1)], axis=0, keepdims=True), (1, 128))
             for kt in range(flag_ref.shape[3])], axis=0)
    oc_ref[0] = _assemble_heads(parts)


def _att_tiles(seq):
    return min(256, seq), min(512, seq)


def _select(q, kc, vc, sf, ovt):
    bsz, seq, d = q.shape
    n = kc.shape[2]
    ns = ovt.shape[0]
    tq, tks = _att_tiles(seq)
    nkt = seq // tks
    nsel = min(SEL_TOPK, ns)
    assert ns % 8 == 0 and ns % nkt == 0
    return pl.pallas_call(
        functools.partial(_sel_body, nsel),
        grid=(bsz, seq // tq),
        in_specs=[pl.BlockSpec((1, tq, d), lambda b, i: (b, i, 0)),
                  pl.BlockSpec((1, NSA_KV, n, KVW), lambda b, i: (b, 0, 0, 0)),
                  pl.BlockSpec((1, NSA_KV, n, 2 * NSA_HD), lambda b, i: (b, 0, 0, 0)),
                  pl.BlockSpec(sf.shape, lambda b, i: (0, 0, 0)),
                  pl.BlockSpec((ns, n), lambda b, i: (0, 0))],
        out_specs=[pl.BlockSpec((1, tq, d), lambda b, i: (b, i, 0)),
                   pl.BlockSpec((1, KVW, tq), lambda b, i: (b, 0, i)),
                   pl.BlockSpec((1, 1, NSA_KV, nkt, 128), lambda b, i: (b, i, 0, 0, 0))],
        out_shape=[jax.ShapeDtypeStruct((bsz, seq, d), F32),
                   jax.ShapeDtypeStruct((bsz, KVW, seq), F32),
                   jax.ShapeDtypeStruct((bsz, seq // tq, NSA_KV, nkt, 128), F32)],
        compiler_params=_params(("arbitrary", "arbitrary")),
        name="nsa_compressed_select",
    )(q, kc, vc, sf, ovt)


def _softmax_tile(s, vb, mask, tq, m_s, acc_s):
    tk = vb.shape[0]
    ps = []
    for h in range(NSA_GROUP):
        rows = slice(h * tq, (h + 1) * tq)
        sh = s[rows]
        if mask is not None:
            sh = jnp.where(mask, sh, NEG_INF)
        m_old = m_s[rows]
        m_new = jnp.maximum(m_old, jnp.max(sh, axis=-1, keepdims=True))
        p = jnp.exp(sh - jnp.concatenate([m_new] * (tk // 128), axis=1))
        m_s[rows] = m_new
        acc_s[rows] = acc_s[rows] * jnp.exp(m_old - m_new)
        ps.append(p.astype(BF16))
    acc_s[...] += _mm(jnp.concatenate(ps, axis=0), vb)


def _finish(g, acc_s):
    one = ((g + 1) % 2) * NSA_HD
    acc = acc_s[...]
    return acc * (1.0 / acc[:, one:one + 1])


def _att_body(tks, flag_ref, q_ref, ks_ref, vs_ref, kw_refs, vw_refs, selm_ref, sf_ref, oc_ref, gate_ref,
              gexp_ref, x_ref, g1_ref, wout_ref, o_ref, m_s, acc_s):
    qi = pl.program_id(1)
    tq = q_ref.shape[1]
    q0 = qi * tq
    nkt = ks_ref.shape[2] // tks
    flag0 = (pl.program_id(0) * pl.num_programs(1) + qi) * (NSA_KV * nkt)
    nwt = len(kw_refs) - 1
    lane = lax.broadcasted_iota(jnp.int32, (1, KVW), 1)
    lane_v = lax.broadcasted_iota(jnp.int32, (1, 2 * NSA_HD), 1)
    ql = lax.broadcasted_iota(jnp.int32, (tq, 1), 0)
    colw = lax.broadcasted_iota(jnp.int32, (1, (nwt + 1) * tq), 1)
    rel = jnp.where(colw // tq + qi >= nwt, colw - nwt * tq - ql, 1)
    wmask = (rel <= 0) & (rel > -WINDOW)
    selm = selm_ref[0]
    o_sel = [[None] * NSA_GROUP for _ in range(NSA_KV)]
    o_win = [[None] * NSA_GROUP for _ in range(NSA_KV)]

    def reset():
        m_s[...] = jnp.full(m_s.shape, NEG_INF, F32)
        acc_s[...] = jnp.zeros(acc_s.shape, F32)

    for g in range(NSA_KV):
        own = _slot(lane, g)
        pick = _slot(lane, (g + 1) % NSA_KV)
        qg = jnp.concatenate(
            [jnp.where(own, q_ref[0, :, h * KVW:(h + 1) * KVW],
                       jnp.where(pick, selm, sf_ref[g * NSA_GROUP + h].astype(BF16)))
             for h in range(NSA_GROUP)], axis=0)

        reset()
        last = (q0 + tq + tks - 1) // tks - 1

        def tile(kt):
            return pl.ds(pl.multiple_of(kt * tks, tks), tks)

        def body(kt, carry):
            @pl.when(flag_ref[flag0 + g * nkt + kt] > 0)
            def _():
                _softmax_tile(_nt(qg, ks_ref[0, g, tile(kt), :]), vs_ref[0, g, tile(kt), :], None, tq,
                              m_s, acc_s)
            return carry

        lax.fori_loop(0, last, body, 0)
        kls = lax.broadcasted_iota(jnp.int32, (1, tks), 1)
        _softmax_tile(_nt(qg, ks_ref[0, g, tile(last), :]), vs_ref[0, g, tile(last), :],
                      (kls + last * tks) <= (ql + q0), tq, m_s, acc_s)
        osg = _finish(g, acc_s)
        for h in range(NSA_GROUP):
            o_sel[g][h] = osg[h * tq:(h + 1) * tq]

        kcat = jnp.concatenate([r[0, g] for r in kw_refs], axis=0)
        vcat = jnp.concatenate([r[0, g] for r in vw_refs], axis=0)
        s = _nt(qg, kcat)
        ps = []
        for h in range(NSA_GROUP):
            sh = jnp.where(wmask, s[h * tq:(h + 1) * tq], NEG_INF)
            ps.append(jnp.exp(sh - jnp.max(sh, axis=-1, keepdims=True)).astype(BF16))
        acc = _mm(jnp.concatenate(ps, axis=0), vcat)
        one = ((g + 1) % 2) * NSA_HD
        owg = acc * (1.0 / acc[:, one:one + 1])
        for h in range(NSA_GROUP):
            o_win[g][h] = owg[h * tq:(h + 1) * tq]

    assemble = _assemble_heads
    gate = gate_ref[0]
    ghi = gate.astype(BF16)
    glo = (gate - ghi.astype(F32)).astype(BF16)
    ghl = jnp.concatenate([ghi, glo], axis=1)
    branches = (oc_ref[0], assemble(o_sel), assemble(o_win))
    o = jnp.zeros(branches[0].shape, F32)
    for c in range(3):
        o = o + _mm(ghl, gexp_ref[c]) * branches[c]
    y = _mm(o.astype(BF16), wout_ref[...])
    o_ref[0] = x_ref[0] + g1_ref[0] * y


def _attend(flags, q, ks, vs, kw, vw, selm, sf, oc, gate, gexp, x, g1, w_out):
    bsz, seq, d = x.shape
    G = NSA_KV
    tq, tks = _att_tiles(seq)
    nwt = WINDOW // tq
    tile = lambda w: pl.BlockSpec((1, tq, w), lambda b, i, f: (b, i, 0))
    kv = lambda w: pl.BlockSpec((1, G, seq, w), lambda b, i, f: (b, 0, 0, 0))
    wtile = lambda w, dj: pl.BlockSpec((1, G, tq, w), lambda b, i, f: (b, 0, jnp.maximum(i - dj, 0), 0))
    full = lambda a: pl.BlockSpec(a.shape, lambda b, i, f: tuple(0 for _ in a.shape))
    kw_specs = [wtile(KVW, dj) for dj in range(nwt, -1, -1)]
    vw_specs = [wtile(2 * NSA_HD, dj) for dj in range(nwt, -1, -1)]

    def body(flag_ref, q_ref, ks_ref, vs_ref, *rest):
        kw_refs = rest[:nwt + 1]
        vw_refs = rest[nwt + 1:2 * nwt + 2]
        _att_body(tks, flag_ref, q_ref, ks_ref, vs_ref, kw_refs, vw_refs, *rest[2 * nwt + 2:])

    return pl.pallas_call(
        body,
        grid_spec=pltpu.PrefetchScalarGridSpec(
            num_scalar_prefetch=1,
            grid=(bsz, seq // tq),
            in_specs=[tile(d), kv(KVW), kv(2 * NSA_HD), *kw_specs, *vw_specs, tile(KVW), full(sf), tile(d),
                      tile(128), full(gexp), tile(d), pl.BlockSpec((1, 1, d), lambda b, i, f: (b, 0, 0)),
                      full(w_out)],
            out_specs=tile(d),
            scratch_shapes=[pltpu.VMEM((NSA_GROUP * tq, 128), F32),
                            pltpu.VMEM((NSA_GROUP * tq, 2 * NSA_HD), F32)]),
        out_shape=jax.ShapeDtypeStruct(x.shape, F32),
        compiler_params=_params(("arbitrary", "arbitrary")),
        name="nsa_attend",
    )(flags, q, ks, vs, *([kw] * (nwt + 1)), *([vw] * (nwt + 1)), selm, sf, oc, gate, gexp, x, g1, w_out)


def _nsa_layer(x, nw, sc, sh, g1, w_in, w_out, cmp_pe, cmp_w1, cmp_w2):
    bsz, seq, d = x.shape
    G, HPG, HD = NSA_KV, NSA_GROUP, NSA_HD
    wq = w_in[:, :d].reshape(d, G, HPG, HD).transpose(0, 2, 1, 3).reshape(d, d)
    ngate = 3 * NSA_HEADS
    wgate = jnp.pad(w_in[:, d + 6 * KVW:], ((0, 0), (0, 128 - ngate)))
    w_cat = jnp.concatenate([wq, w_in[:, d:d + 6 * KVW], wgate], axis=1).astype(BF16)
    w_out_p = w_out.reshape(G, HPG, HD, d).transpose(1, 0, 2, 3).reshape(d, d).astype(BF16)

    ns = seq // SEL_LEN
    assert ns <= HD and CMP_LEN == 2 * CMP_STRIDE
    pos = np.arange(seq)
    fks = np.zeros((G, seq, KVW), np.float32)
    fkw = np.zeros((G, seq, KVW), np.float32)
    fv = np.zeros((G, 1, 2 * HD), np.float32)
    sf = np.zeros((G * HPG, 1, KVW), np.float32)
    slopes = _alibi_slopes()
    for g in range(G):
        pick0, feat0 = ((g + 1) % G) * HD, ((g + 2) % G) * HD
        fks[g, pos, pick0 + pos // SEL_LEN] = 1.0
        for arr in (fks, fkw):
            arr[g, :, feat0 + 0] = arr[g, :, feat0 + 2] = (pos // 64) * 64
            arr[g, :, feat0 + 1] = arr[g, :, feat0 + 3] = pos % 64
        fv[g, 0, ((g + 1) % 2) * HD] = 1.0
        for h in range(HPG):
            s32 = np.float32(slopes[g][h])
            hi = s32.astype(BF16).astype(np.float32)
            lo = np.float32(s32 - hi).astype(BF16).astype(np.float32)
            sf[g * HPG + h, 0, feat0:feat0 + 4] = (hi, hi, lo, lo)
    q, kvc, ks, vs, kw, vw, gate = _nsa_in(x, nw, sc, sh, w_cat, jnp.asarray(fks, BF16),
                                           jnp.asarray(fkw, BF16), jnp.asarray(fv))

    nseg = seq // CMP_STRIDE

    def strides(t):
        return t.reshape(bsz, nseg, CMP_STRIDE, G, HD).transpose(0, 3, 1, 2, 4).reshape(
            bsz, G, nseg, CMP_STRIDE * HD)

    pe = cmp_pe.reshape(2, 2, CMP_STRIDE * HD)
    w1 = cmp_w1.reshape(2, 2, CMP_STRIDE * HD, HD).astype(BF16)
    kc, vc = _compress(strides(kvc[..., :KVW]), strides(kvc[..., KVW:]), pe, w1, cmp_w2.astype(BF16))
    mid = np.zeros((nseg, HD), np.float32)
    mid[:, 0] = mid[:, 2] = np.arange(nseg) * CMP_STRIDE
    mid[:, 1] = mid[:, 3] = (CMP_LEN - 1) / 2.0
    zero = jnp.zeros((bsz, nseg, HD), BF16)
    midb = jnp.broadcast_to(jnp.asarray(mid, BF16), (bsz, nseg, HD))
    kce, vce = [], []
    for g in range(G):
        slots = [zero] * G
        slots[g] = kc[:, g].astype(BF16)
        slots[(g + 2) % G] = midb
        kce.append(jnp.concatenate(slots, axis=-1))
        half = [zero, zero]
        half[g % 2] = vc[:, g].astype(BF16)
        vce.append(jnp.concatenate(half, axis=-1))
    kce = jnp.stack(kce, axis=1)
    vce = jnp.stack(vce, axis=1)

    ci = np.arange(nseg)[:, None] * CMP_STRIDE
    sj = np.arange(ns)[None, :] * SEL_LEN
    overlap = ((ci <= sj + SEL_LEN - 1) & (ci + CMP_LEN - 1 >= sj)
               & (np.arange(nseg)[:, None] < nseg - 1))
    ovt = jnp.asarray(overlap.T, F32)
    sf = jnp.asarray(sf)
    oc, selt, tile_any = _select(q, kce, vce, sf, ovt)
    selm = selt.transpose(0, 2, 1).astype(BF16)
    flags = (tile_any[..., 0] > 0.5).astype(jnp.int32).reshape(-1)

    gexp = np.zeros((3, 256, d), np.float32)
    for g in range(G):
        for h in range(HPG):
            for c in range(3):
                for half in range(2):
                    gexp[c, half * 128 + g * HPG * 3 + h * 3 + c, (h * G + g) * HD:(h * G + g + 1) * HD] = 1.0
    return _attend(flags, q, ks, vs, kw, vw, selm, sf, oc, gate, jnp.asarray(gexp, BF16), x, g1,
                   w_out_p)


def kernel(x, c, ada_w, ada_b, norm_mix, norm_ffn, final_norm, hg_w_in, hg_w_out, hg_gnorm, hg_lb,
           nsa_w_in, nsa_w_out, nsa_cmp_pe, nsa_cmp_w1, nsa_cmp_w2, ffn_w_up, ffn_conv_w, ffn_conv_b,
           ffn_w_down):
    depth = ada_w.shape[0]
    d = x.shape[-1]
    mod = _ada(c, ada_w, ada_b)
    for layer in range(depth):
        sh1, sc1, g1, sh2, sc2, g2 = [mod[layer, :, None, k * d:(k + 1) * d] for k in range(6)]
        j = layer // 2
        if layer % 2 == 0:
            x = _hgrn_layer(x, norm_mix[layer], sc1, sh1, g1, hg_w_in[j], hg_w_out[j], hg_gnorm[j], hg_lb, j)
        else:
            x = _nsa_layer(x, norm_mix[layer], sc1, sh1, g1, nsa_w_in[j], nsa_w_out[j], nsa_cmp_pe[j],
                           nsa_cmp_w1[j], nsa_cmp_w2[j])
        x = _ffn_layer(x, norm_ffn[layer], sc2, sh2, g2, ffn_w_up[layer], ffn_conv_w[layer],
                       ffn_conv_b[layer], ffn_w_down[layer], final_norm, layer == depth - 1)
    return x
```

```python
import functools

import numpy as np
import jax
import jax.numpy as jnp
from jax import lax
from jax.experimental import pallas as pl
from jax.experimental.pallas import tpu as pltpu

F32 = jnp.float32
BF16 = jnp.bfloat16
HIGHEST = lax.Precision.HIGHEST

EPS = 1e-6
NEG_INF = -1e30
LOG2_E = 1.4426950408889634
HG_HEADS = 8
HG_DK = 128
HG_CHUNK = 64
HG_SUB = 8
NSA_HEADS = 16
NSA_KV = 4
NSA_GROUP = NSA_HEADS // NSA_KV
NSA_HD = 64
CMP_LEN = 32
CMP_STRIDE = 16
SEL_LEN = 64
SEL_TOPK = 16
WINDOW = 512
FORCED_SCORE = 1e4
CONV_W = 3
KVW = NSA_KV * NSA_HD

VMEM_LIMIT_BYTES = 56 * 1024 * 1024
FFN_HALO = 16


def _mm(a, b):
    return jnp.dot(a, b, preferred_element_type=F32)


def _nt(a, b):
    return lax.dot_general(a, b, (((1,), (1,)), ((), ())), preferred_element_type=F32)


def _tn(a, b):
    return lax.dot_general(a, b, (((0,), (0,)), ((), ())), preferred_element_type=F32)


def _sigmoid(x):
    return 1.0 / (1.0 + jnp.exp(-x))


def _norm_mod(x, nw, sc, sh):
    ms = jnp.mean(x * x, axis=-1, keepdims=True)
    return (x * lax.rsqrt(ms + EPS) * nw) * (1.0 + sc) + sh


def _params(sem):
    return pltpu.CompilerParams(dimension_semantics=sem, vmem_limit_bytes=VMEM_LIMIT_BYTES)


def _ada_body(c_ref, w_ref, b_ref, o_ref):
    c = c_ref[...]
    ca = c * _sigmoid(c)
    o_ref[0] = jnp.dot(ca, w_ref[0], preferred_element_type=F32, precision=HIGHEST) + b_ref[0]


def _ada(c, ada_w, ada_b):
    depth, d, n6 = ada_w.shape
    bsz = c.shape[0]
    tn = n6 // 4
    return pl.pallas_call(
        _ada_body,
        grid=(depth, n6 // tn),
        in_specs=[pl.BlockSpec((bsz, d), lambda l, j: (0, 0)),
                  pl.BlockSpec((1, d, tn), lambda l, j: (l, 0, j)),
                  pl.BlockSpec((1, 1, tn), lambda l, j: (l, 0, j))],
        out_specs=pl.BlockSpec((1, bsz, tn), lambda l, j: (l, 0, j)),
        out_shape=jax.ShapeDtypeStruct((depth, bsz, n6), F32),
        compiler_params=_params(("arbitrary", "arbitrary")),
        name="ada_mod",
    )(c, ada_w, ada_b.reshape(depth, 1, n6))


def _pair_offset(j, i):
    nb = HG_CHUNK // HG_SUB
    before = sum(nb - 1 - jj for jj in range(j))
    return HG_SUB * (before + (i - j - 1))


def _hgrn_scale_stage(q_s, k_s, b2_s, b_s, r0, ls):
    C, SB = HG_CHUNK, HG_SUB
    nb = C // SB
    rows = pl.ds(r0, C)
    q = q_s[rows, ls]
    b = b2_s[rows, ls]
    b_s[0, :, ls] = b
    b_s[1, :, ls] = k_s[rows, ls]
    bq = [b[SB * i:SB * (i + 1)] for i in range(nb)]
    qq = [q[SB * i:SB * (i + 1)] for i in range(nb)]
    bend = [b_s[0, SB * j + SB - 1:SB * j + SB, ls] for j in range(nb)]
    lhs = []
    for j in range(nb - 1):
        for i in range(j + 1, nb):
            lhs.append(qq[i] * jnp.exp2(bq[i] - bend[j]))
    prod = []
    for i in range(nb):
        for s in range(SB):
            r = SB * i + s
            brow = b_s[0, r:r + 1, ls]
            krow = b_s[1, r:r + 1, ls]
            arg = bq[i] - brow
            if s > 0:
                arg = jnp.minimum(arg, 0.0)
            prod.append(qq[i] * krow * jnp.exp2(arg))
    return jnp.concatenate(lhs, axis=0).astype(BF16), jnp.concatenate(prod, axis=0).astype(BF16)


def _hgrn_score_stage(scaled, kt_s, r0, ls, ones_r, lane, lane_blk, causal):
    C, SB = HG_CHUNK, HG_SUB
    nb = C // SB
    lhs, prod = scaled
    res = _nt(lhs, kt_s[pl.ds(r0, C), ls])
    red = _mm(prod, ones_r)
    srows = []
    for i in range(nb):
        s_i = jnp.zeros((SB, C), F32)
        for j in range(i):
            off = _pair_offset(j, i)
            s_i = jnp.where(lane_blk == j, res[off:off + SB], s_i)
        for s in range(SB):
            r = SB * i + s
            s_i = jnp.where(lane == r, red[SB * r:SB * (r + 1)], s_i)
        srows.append(s_i)
    return jnp.where(causal, jnp.concatenate(srows, axis=0), 0.0).astype(BF16)


def _hgrn_output_stage(scores, v_s, qd_s, kd_s, dec_s, st_s, hidx, c, r0, ls):
    rows = pl.ds(r0, HG_CHUNK)
    v = v_s[rows, ls]
    st = st_s[hidx]
    o = _nt(qd_s[rows, ls], st.astype(BF16)) + _mm(scores, v)
    st_s[hidx] = st * dec_s[c, 0:1, ls] + _tn(v, kd_s[rows, ls])
    return o


def _hgrn_body(lb_row, x_ref, nw_ref, sc_ref, sh_ref, g1_ref, wq_ref, wf_ref, wi_ref, wg_ref, wout_ref,
               gn_ref, lb_ref, o_ref, h_s, q_s, k_s, b2_s, v_s, qd_s, kt_s, kd_s, dec_s, b_s, sc_s, oh_s, og_s,
               st_s):
    si = pl.program_id(1)
    hp = pl.program_id(2)
    ts = x_ref.shape[1]
    width = q_s.shape[1]
    hpb = width // HG_DK
    C, SB = HG_CHUNK, HG_SUB

    @pl.when(si == 0)
    def _():
        for hh in range(hpb):
            st_s[hp * hpb + hh] = jnp.zeros((HG_DK, HG_DK), F32)

    @pl.when(hp == 0)
    def _():
        h_s[...] = _norm_mod(x_ref[0], nw_ref[...], sc_ref[0], sh_ref[0]).astype(BF16)

    h = h_s[...]
    lbraw = lb_ref[...]
    e = jnp.exp(lbraw - jnp.max(lbraw, axis=0, keepdims=True))
    lb = jnp.sum(e[:lb_row + 1], axis=0, keepdims=True) / jnp.sum(e, axis=0, keepdims=True)
    q_s[...] = _mm(h, wq_ref[...])
    f = _mm(h, wf_ref[...])
    ef = jnp.exp(-jnp.abs(f))
    rcp = 1.0 / (1.0 + ef)
    pos = f >= 0
    sig_p = jnp.where(pos, rcp, ef * rcp)
    sig_n = jnp.where(pos, ef * rcp, rcp)
    lf2 = jnp.log(lb + (1.0 - lb) * sig_p) * LOG2_E
    k_s[...] = (1.0 - lb) * sig_n
    v_s[...] = _mm(h, wi_ref[...]).astype(BF16)

    ri = lax.broadcasted_iota(jnp.int32, (C, C), 0)
    ci = lax.broadcasted_iota(jnp.int32, (C, C), 1)
    causal = ri >= ci
    tril = jnp.where(causal, 1.0, 0.0).astype(BF16)
    hi = lf2.astype(BF16)
    r1 = lf2 - hi.astype(F32)
    mid = r1.astype(BF16)
    lo = (r1 - mid.astype(F32)).astype(BF16)
    pieces = jnp.concatenate([hi, mid, lo], axis=1)
    for c in range(ts // C):
        rows = slice(c * C, (c + 1) * C)
        cs = _mm(tril, pieces[rows])
        b = cs[:, :width] + cs[:, width:2 * width] + cs[:, 2 * width:]
        bend = jnp.concatenate([jnp.broadcast_to(b[j + SB - 1:j + SB], (SB, width)) for j in range(0, C, SB)],
                               axis=0)
        blast = jnp.broadcast_to(b[C - 1:C], (C, width))
        k = k_s[rows]
        b2_s[rows] = b
        qd_s[rows] = (q_s[rows] * jnp.exp2(b)).astype(BF16)
        kt_s[rows] = (k * jnp.exp2(bend - b)).astype(BF16)
        kd_s[rows] = (k * jnp.exp2(blast - b)).astype(BF16)
        dec_s[c] = jnp.exp2(blast[0:SB])

    ones_r = jnp.ones((HG_DK, C), BF16)
    lane = lax.broadcasted_iota(jnp.int32, (HG_SUB, C), 1)
    lane_blk = lane // HG_SUB

    lanes = lambda hh: slice(hh * HG_DK, (hh + 1) * HG_DK)

    def scores_of(c):
        r0 = c * C if isinstance(c, int) else pl.multiple_of(c * C, C)
        for hh in range(hpb):
            scaled = _hgrn_scale_stage(q_s, k_s, b2_s, b_s, r0, lanes(hh))
            sc_s[c % 2, hh] = _hgrn_score_stage(scaled, kt_s, r0, lanes(hh), ones_r, lane, lane_blk, causal)

    def outputs_of(c):
        r0 = c * C if isinstance(c, int) else pl.multiple_of(c * C, C)
        for hh in range(hpb):
            oh_s[pl.ds(r0, C), lanes(hh)] = _hgrn_output_stage(
                sc_s[c % 2, hh], v_s, qd_s, kd_s, dec_s, st_s, hp * hpb + hh, c, r0, lanes(hh))

    nchunk = ts // C
    scores_of(0)

    def chunk(c, carry):
        scores_of(c + 1)
        outputs_of(c)
        return carry

    lax.fori_loop(0, nchunk - 1, chunk, 0)
    outputs_of(nchunk - 1)

    g = _mm(h, wg_ref[...])
    gate = g * _sigmoid(g)
    gn = gn_ref[...]
    parts = []
    for hh in range(hpb):
        ls = slice(hh * HG_DK, (hh + 1) * HG_DK)
        oh = oh_s[:, ls]
        ms = jnp.mean(oh * oh, axis=-1, keepdims=True)
        parts.append(oh * lax.rsqrt(ms + EPS) * gn)
    og_s[hp] = (jnp.concatenate(parts, axis=1) * gate).astype(BF16)

    nblk = og_s.shape[0]

    @pl.when(hp == nblk - 1)
    def _():
        y = jnp.zeros(o_ref.shape[1:], F32)
        for blk in range(nblk):
            y = y + _mm(og_s[blk], wout_ref[blk * width:(blk + 1) * width, :])
        o_ref[0] = x_ref[0] + g1_ref[0] * y


def _hgrn_layer(x, nw, sc, sh, g1, w_in, w_out, gnorm, hg_lb, lb_row):
    bsz, seq, d = x.shape
    ts = min(512, seq)
    hpb = HG_HEADS
    width = hpb * HG_DK
    nblk = d // width
    w_in = w_in.astype(BF16)
    seg = lambda k: pl.BlockSpec((d, width), lambda b, s, hp, k=k: (0, k * nblk + hp))
    vec = pl.BlockSpec((1, 1, d), lambda b, s, hp: (b, 0, 0))
    return pl.pallas_call(
        functools.partial(_hgrn_body, lb_row),
        grid=(bsz, seq // ts, nblk),
        in_specs=[pl.BlockSpec((1, ts, d), lambda b, s, hp: (b, s, 0)),
                  pl.BlockSpec((1, d), lambda b, s, hp: (0, 0)),
                  vec, vec, vec,
                  seg(0), seg(1), seg(2), seg(3),
                  pl.BlockSpec((d, d), lambda b, s, hp: (0, 0)),
                  pl.BlockSpec((1, HG_DK), lambda b, s, hp: (0, 0)),
                  pl.BlockSpec((hg_lb.shape[0], width), lambda b, s, hp: (0, hp))],
        out_specs=pl.BlockSpec((1, ts, d), lambda b, s, hp: (b, s, 0)),
        out_shape=jax.ShapeDtypeStruct(x.shape, F32),
        scratch_shapes=[pltpu.VMEM((ts, d), BF16),
                        pltpu.VMEM((ts, width), F32), pltpu.VMEM((ts, width), F32),
                        pltpu.VMEM((ts, width), F32), pltpu.VMEM((ts, width), BF16),
                        pltpu.VMEM((ts, width), BF16), pltpu.VMEM((ts, width), BF16),
                        pltpu.VMEM((ts, width), BF16),
                        pltpu.VMEM((ts // HG_CHUNK, HG_SUB, width), F32),
                        pltpu.VMEM((2, HG_CHUNK, width), F32),
                        pltpu.VMEM((2, hpb, HG_CHUNK, HG_CHUNK), BF16),
                        pltpu.VMEM((ts, width), F32),
                        pltpu.VMEM((nblk, ts, width), BF16),
                        pltpu.VMEM((HG_HEADS, HG_DK, HG_DK), F32)],
        compiler_params=_params(("arbitrary", "arbitrary", "arbitrary")),
        name="hgrn2_layer",
    )(x, nw.reshape(1, d), sc, sh, g1, w_in, w_in, w_in, w_in, w_out.astype(BF16),
      gnorm.reshape(1, HG_DK), hg_lb)


def _ffn_body(final, nchunk, x_ref, xh_ref, nw_ref, sc_ref, sh_ref, g2_ref, wa_ref, wv_ref, cw_ref, cb_ref,
              wd_ref, fn_ref, o_ref, hx_s, a_s):
    i = pl.program_id(1)
    tm = x_ref.shape[1]
    H = FFN_HALO
    x = x_ref[0]
    nw, sc, sh = nw_ref[...], sc_ref[0], sh_ref[0]
    hx_s[H:, :] = _norm_mod(x, nw, sc, sh).astype(BF16)
    hx_s[0:H, :] = jnp.where(i > 0, _norm_mod(xh_ref[0], nw, sc, sh), 0.0).astype(BF16)
    ff = wa_ref.shape[1]
    fc = ff // nchunk
    y = jnp.zeros((tm, x.shape[1]), F32)
    for c in range(nchunk):
        cols = slice(c * fc, (c + 1) * fc)
        a_s[...] = _mm(hx_s[...], wa_ref[:, cols])
        v = _mm(hx_s[H:, :], wv_ref[:, cols])
        cw = cw_ref[:, cols]
        conv = (cw[0:1] * a_s[H - 2:H - 2 + tm, :] + cw[1:2] * a_s[H - 1:H - 1 + tm, :]
                + cw[2:3] * a_s[H:H + tm, :] + cb_ref[:, cols])
        u = (conv * _sigmoid(conv) * v).astype(BF16)
        y = y + _mm(u, wd_ref[cols, :])
    out = x + g2_ref[0] * y
    if final:
        ms = jnp.mean(out * out, axis=-1, keepdims=True)
        out = out * lax.rsqrt(ms + EPS) * fn_ref[...]
    o_ref[0] = out


def _ffn_layer(x, nw, sc, sh, g2, w_up, conv_w, conv_b, w_down, final_norm, final):
    bsz, seq, d = x.shape
    ff = w_down.shape[0]
    tm = min(512, seq)
    nchunk = 2
    H = FFN_HALO
    wa = w_up[:, :ff].astype(BF16)
    wv = w_up[:, ff:].astype(BF16)
    vec = pl.BlockSpec((1, 1, d), lambda b, i: (b, 0, 0))
    full = lambda shape: pl.BlockSpec(shape, lambda b, i: tuple(0 for _ in shape))
    return pl.pallas_call(
        functools.partial(_ffn_body, final, nchunk),
        grid=(bsz, seq // tm),
        in_specs=[pl.BlockSpec((1, tm, d), lambda b, i: (b, i, 0)),
                  pl.BlockSpec((1, H, d), lambda b, i: (b, jnp.maximum(i * (tm // H) - 1, 0), 0)),
                  full((1, d)), vec, vec, vec,
                  full((d, ff)), full((d, ff)), full((CONV_W, ff)), full((1, ff)), full((ff, d)),
                  full((1, d))],
        out_specs=pl.BlockSpec((1, tm, d), lambda b, i: (b, i, 0)),
        out_shape=jax.ShapeDtypeStruct(x.shape, F32),
        scratch_shapes=[pltpu.VMEM((tm + H, d), BF16), pltpu.VMEM((tm + H, ff // nchunk), F32)],
        compiler_params=_params(("arbitrary", "arbitrary")),
        name="conv_ffn_layer",
    )(x, x, nw.reshape(1, d), sc, sh, g2, wa, wv, conv_w, conv_b.reshape(1, ff), w_down.astype(BF16),
      final_norm.reshape(1, d))


def _alibi_slopes():
    return [[2.0 ** (-8.0 * (g * NSA_GROUP + h + 1) / NSA_HEADS) for h in range(NSA_GROUP)]
            for g in range(NSA_KV)]


def _slot(lane, g):
    return (lane >= g * NSA_HD) & (lane < (g + 1) * NSA_HD)


def _nsa_in_body(x_ref, nw_ref, sc_ref, sh_ref, w_ref, fks_ref, fkw_ref, fv_ref,
                 q_ref, kvc_ref, ks_ref, vs_ref, kw_ref, vw_ref, gate_ref):
    d = x_ref.shape[2]
    h = _norm_mod(x_ref[0], nw_ref[...], sc_ref[0], sh_ref[0]).astype(BF16)
    p = _mm(h, w_ref[...])
    q_ref[0] = (p[:, :d] * (NSA_HD ** -0.5)).astype(BF16)
    o = d
    kvc_ref[0] = p[:, o:o + 2 * KVW]
    o += 2 * KVW
    ksb, vsb, kwb, vwb = [p[:, o + i * KVW:o + (i + 1) * KVW].astype(BF16) for i in range(4)]
    o += 4 * KVW
    lane = lax.broadcasted_iota(jnp.int32, (1, KVW), 1)
    lane_v = lax.broadcasted_iota(jnp.int32, (1, 2 * NSA_HD), 1)
    for g in range(NSA_KV):
        own = _slot(lane, g)
        ks_ref[0, g] = jnp.where(own, ksb, fks_ref[g])
        kw_ref[0, g] = jnp.where(own, kwb, fkw_ref[g])
        half = slice((g // 2) * 2 * NSA_HD, (g // 2 + 1) * 2 * NSA_HD)
        own_v = _slot(lane_v, g % 2)
        fv = fv_ref[g].astype(BF16)
        vs_ref[0, g] = jnp.where(own_v, vsb[:, half], fv)
        vw_ref[0, g] = jnp.where(own_v, vwb[:, half], fv)
    gate_ref[0] = _sigmoid(p[:, o:])


def _nsa_in(x, nw, sc, sh, w_cat, fks, fkw, fv):
    bsz, seq, d = x.shape
    tm = min(512, seq)
    ncol = w_cat.shape[1]
    G = NSA_KV
    tile = lambda w: pl.BlockSpec((1, tm, w), lambda b, i: (b, i, 0))
    gtile = lambda w: pl.BlockSpec((1, G, tm, w), lambda b, i: (b, 0, i, 0))
    vec = pl.BlockSpec((1, 1, d), lambda b, i: (b, 0, 0))
    sds = lambda w, dt: jax.ShapeDtypeStruct((bsz, seq, w), dt)
    gsds = lambda w: jax.ShapeDtypeStruct((bsz, G, seq, w), BF16)
    feat = pl.BlockSpec((G, tm, KVW), lambda b, i: (0, i, 0))
    return pl.pallas_call(
        _nsa_in_body,
        grid=(bsz, seq // tm),
        in_specs=[tile(d), pl.BlockSpec((1, d), lambda b, i: (0, 0)), vec, vec,
                  pl.BlockSpec((d, ncol), lambda b, i: (0, 0)), feat, feat,
                  pl.BlockSpec((G, 1, 2 * NSA_HD), lambda b, i: (0, 0, 0))],
        out_specs=[tile(d), tile(2 * KVW), gtile(KVW), gtile(2 * NSA_HD), gtile(KVW), gtile(2 * NSA_HD),
                   tile(128)],
        out_shape=[sds(d, BF16), sds(2 * KVW, F32), gsds(KVW), gsds(2 * NSA_HD), gsds(KVW),
                   gsds(2 * NSA_HD), sds(128, F32)],
        compiler_params=_params(("arbitrary", "arbitrary")),
        name="nsa_in_proj",
    )(x, nw.reshape(1, d), sc, sh, w_cat, fks, fkw, fv)


def _cmp_body(kseg_ref, vseg_ref, pe_ref, w1_ref, w2_ref, kc_ref, vc_ref):
    nseg = kseg_ref.shape[2]
    row = lax.broadcasted_iota(jnp.int32, (nseg, NSA_HD), 0)
    for i, (src, dst) in enumerate(((kseg_ref, kc_ref), (vseg_ref, vc_ref))):
        seg = src[0, 0]
        a = _mm((seg + pe_ref[i, 0:1, :]).astype(BF16), w1_ref[i, 0])
        bm = _mm((seg + pe_ref[i, 1:2, :]).astype(BF16), w1_ref[i, 1])
        pre = a + pltpu.roll(bm, nseg - 1, 0)
        act = pre * _sigmoid(pre)
        out = _mm(act.astype(BF16), w2_ref[i])
        dst[0, 0] = jnp.where(row < nseg - 1, out, 0.0)


def _compress(kseg, vseg, pe, w1, w2):
    bsz, ng, nseg, width = kseg.shape
    blk = pl.BlockSpec((1, 1, nseg, width), lambda b, g: (b, g, 0, 0))
    oblk = pl.BlockSpec((1, 1, nseg, NSA_HD), lambda b, g: (b, g, 0, 0))
    full = lambda a: pl.BlockSpec(a.shape, lambda b, g: tuple(0 for _ in a.shape))
    sds = jax.ShapeDtypeStruct((bsz, ng, nseg, NSA_HD), F32)
    return pl.pallas_call(
        _cmp_body,
        grid=(bsz, ng),
        in_specs=[blk, blk, full(pe), full(w1), full(w2)],
        out_specs=[oblk, oblk],
        out_shape=[sds, sds],
        compiler_params=_params(("arbitrary", "arbitrary")),
        name="nsa_compress",
    )(kseg, vseg, pe, w1, w2)


def _assemble_heads(parts):
    tq = parts[0][0].shape[0]
    first = lax.broadcasted_iota(jnp.int32, (1, 2 * NSA_HD), 1) < NSA_HD
    cols = []
    for h in range(NSA_GROUP):
        for pair in range(NSA_KV // 2):
            cols.append(jnp.where(first, parts[2 * pair][h], parts[2 * pair + 1][h]))
    return jnp.concatenate(cols, axis=1)


def _sel_body(nsel, q_ref, kc_ref, vc_ref, sf_ref, ovt_ref, oc_ref, selt_ref, flag_ref):
    qi = pl.program_id(1)
    tq = q_ref.shape[1]
    n = kc_ref.shape[2]
    ns = ovt_ref.shape[0]
    q0 = qi * tq
    SB = 8
    tcol = q0 + lax.broadcasted_iota(jnp.int32, (tq, 1), 0)
    ncol = lax.broadcasted_iota(jnp.int32, (1, n), 1)
    valid = (ncol * CMP_STRIDE + (CMP_LEN - 1)) <= tcol
    rowok = jnp.where(tcol >= CMP_LEN - 1, 1.0, 0.0)
    lane = lax.broadcasted_iota(jnp.int32, (1, KVW), 1)
    jidx = lax.broadcasted_iota(jnp.int32, (ns, tq), 0)
    jloc = lax.broadcasted_iota(jnp.int32, (SB, tq), 0)
    tl = q0 + lax.broadcasted_iota(jnp.int32, (ns, tq), 1)
    cur = tl // SEL_LEN
    valid_s = jidx * SEL_LEN <= tl
    forced = ((jidx == 0) | (jidx == cur) | (jidx == cur - 1)) & valid_s
    parts = [[None] * NSA_GROUP for _ in range(NSA_KV)]
    for g in range(NSA_KV):
        own = _slot(lane, g)
        psum = jnp.zeros((tq, n), F32)
        ps = []
        for h in range(NSA_GROUP):
            qe = jnp.where(own, q_ref[0, :, h * KVW:(h + 1) * KVW], sf_ref[g * NSA_GROUP + h].astype(BF16))
            s = jnp.where(valid, _nt(qe, kc_ref[0, g]), NEG_INF)
            ex = jnp.exp(s - jnp.max(s, axis=-1, keepdims=True))
            p = ex * (rowok / jnp.sum(ex, axis=-1, keepdims=True))
            psum = psum + p
            ps.append(p.astype(BF16))
        og = _mm(jnp.concatenate(ps, axis=0), vc_ref[0, g])
        for h in range(NSA_GROUP):
            parts[g][h] = og[h * tq:(h + 1) * tq]
        imp = lax.dot_general(ovt_ref[...], psum, (((1,), (1,)), ((), ())),
                              preferred_element_type=F32, precision=HIGHEST)
        score = jnp.where(forced, FORCED_SCORE, jnp.where(valid_s, imp, -1.0))
        blocks = [score[SB * k:SB * (k + 1)] for k in range(ns // SB)]
        cnts = [jnp.zeros((SB, tq), F32) for _ in blocks]
        for jp in range(ns):
            row = score[jp:jp + 1, :]
            for k, blk in enumerate(blocks):
                if SB * k > jp:
                    beats = jnp.where(row >= blk, 1.0, 0.0)
                elif SB * (k + 1) <= jp:
                    beats = jnp.where(row > blk, 1.0, 0.0)
                else:
                    beats = jnp.where(jloc > jp - SB * k, jnp.where(row >= blk, 1.0, 0.0),
                                      jnp.where(row > blk, 1.0, 0.0))
                cnts[k] = cnts[k] + beats
        cnt = jnp.concatenate(cnts, axis=0)
        r0 = ((g + 1) % NSA_KV) * NSA_HD
        selt_ref[0, r0:r0 + ns, :] = jnp.where(cnt < nsel, 0.0, NEG_INF)
        if ns < NSA_HD:
            selt_ref[0, r0 + ns:r0 + NSA_HD, :] = jnp.zeros((NSA_HD - ns, tq), F32)
        anyq = jnp.max(jnp.where(cnt < nsel, 1.0, 0.0), axis=1, keepdims=True)
        bpt = ns // flag_ref.shape[3]
        flag_ref[0, 0, g] = jnp.concatenate(
            [jnp.broadcast_to(jnp.max(anyq[bpt * kt:bpt * (kt + 1)], axis=0, keepdims=True), (1, 128))
             for kt in range(flag_ref.shape[3])], axis=0)
    oc_ref[0] = _assemble_heads(parts)


def _att_tiles(seq):
    return min(256, seq), min(512, seq)


def _select(q, kc, vc, sf, ovt):
    bsz, seq, d = q.shape
    n = kc.shape[2]
    ns = ovt.shape[0]
    tq, tks = _att_tiles(seq)
    nkt = seq // tks
    nsel = min(SEL_TOPK, ns)
    assert ns % 8 == 0 and ns % nkt == 0
    return pl.pallas_call(
        functools.partial(_sel_body, nsel),
        grid=(bsz, seq // tq),
        in_specs=[pl.BlockSpec((1, tq, d), lambda b, i: (b, i, 0)),
                  pl.BlockSpec((1, NSA_KV, n, KVW), lambda b, i: (b, 0, 0, 0)),
                  pl.BlockSpec((1, NSA_KV, n, 2 * NSA_HD), lambda b, i: (b, 0, 0, 0)),
                  pl.BlockSpec(sf.shape, lambda b, i: (0, 0, 0)),
                  pl.BlockSpec((ns, n), lambda b, i: (0, 0))],
        out_specs=[pl.BlockSpec((1, tq, d), lambda b, i: (b, i, 0)),
                   pl.BlockSpec((1, KVW, tq), lambda b, i: (b, 0, i)),
                   pl.BlockSpec((1, 1, NSA_KV, nkt, 128), lambda b, i: (b, i, 0, 0, 0))],
        out_shape=[jax.ShapeDtypeStruct((bsz, seq, d), F32),
                   jax.ShapeDtypeStruct((bsz, KVW, seq), F32),
                   jax.ShapeDtypeStruct((bsz, seq // tq, NSA_KV, nkt, 128), F32)],
        compiler_params=_params(("arbitrary", "arbitrary")),
        name="nsa_compressed_select",
    )(q, kc, vc, sf, ovt)


def _softmax_tile(s, vb, mask, tq, m_s, acc_s):
    tk = vb.shape[0]
    ps = []
    for h in range(NSA_GROUP):
        rows = slice(h * tq, (h + 1) * tq)
        sh = s[rows]
        if mask is not None:
            sh = jnp.where(mask, sh, NEG_INF)
        m_old = m_s[rows]
        m_new = jnp.maximum(m_old, jnp.max(sh, axis=-1, keepdims=True))
        p = jnp.exp(sh - jnp.concatenate([m_new] * (tk // 128), axis=1))
        m_s[rows] = m_new
        acc_s[rows] = acc_s[rows] * jnp.exp(m_old - m_new)
        ps.append(p.astype(BF16))
    acc_s[...] += _mm(jnp.concatenate(ps, axis=0), vb)


def _finish(g, acc_s):
    one = ((g + 1) % 2) * NSA_HD
    acc = acc_s[...]
    return acc * (1.0 / acc[:, one:one + 1])


def _att_body(tks, flag_ref, q_ref, ks_ref, vs_ref, kw_refs, vw_refs, selm_ref, sf_ref, oc_ref, gate_ref,
              gexp_ref, x_ref, g1_ref, wout_ref, o_ref, m_s, acc_s, sd_s, sw_s):
    qi = pl.program_id(1)
    tq = q_ref.shape[1]
    q0 = qi * tq
    nkt = ks_ref.shape[2] // tks
    flag0 = (pl.program_id(0) * pl.num_programs(1) + qi) * (NSA_KV * nkt)
    nwt = len(kw_refs) - 1
    lane = lax.broadcasted_iota(jnp.int32, (1, KVW), 1)
    lane_v = lax.broadcasted_iota(jnp.int32, (1, 2 * NSA_HD), 1)
    ql = lax.broadcasted_iota(jnp.int32, (tq, 1), 0)
    colw = lax.broadcasted_iota(jnp.int32, (1, (nwt + 1) * tq), 1)
    rel = jnp.where(colw // tq + qi >= nwt, colw - nwt * tq - ql, 1)
    wmask = (rel <= 0) & (rel > -WINDOW)
    selm = selm_ref[0]
    o_sel = [[None] * NSA_GROUP for _ in range(NSA_KV)]
    o_win = [[None] * NSA_GROUP for _ in range(NSA_KV)]

    def reset():
        m_s[...] = jnp.full(m_s.shape, NEG_INF, F32)
        acc_s[...] = jnp.zeros(acc_s.shape, F32)

    last = (q0 + tq + tks - 1) // tks - 1
    diag = pl.ds(pl.multiple_of(last * tks, tks), tks)
    dmask = (lax.broadcasted_iota(jnp.int32, (1, tks), 1) + last * tks) <= (ql + q0)

    def queries(g):
        own = _slot(lane, g)
        pick = _slot(lane, (g + 1) % NSA_KV)
        return jnp.concatenate(
            [jnp.where(own, q_ref[0, :, h * KVW:(h + 1) * KVW],
                       jnp.where(pick, selm, sf_ref[g * NSA_GROUP + h].astype(BF16)))
             for h in range(NSA_GROUP)], axis=0)

    def static_scores(g, qg):
        sd_s[g % 2] = _nt(qg, ks_ref[0, g, diag, :])
        sw_s[g % 2] = _nt(qg, jnp.concatenate([r[0, g] for r in kw_refs], axis=0))

    def static_outputs(g):
        _softmax_tile(sd_s[g % 2], vs_ref[0, g, diag, :], dmask, tq, m_s, acc_s)
        osg = _finish(g, acc_s)
        ps = []
        for h in range(NSA_GROUP):
            sh = jnp.where(wmask, sw_s[g % 2, h * tq:(h + 1) * tq], NEG_INF)
            ps.append(jnp.exp(sh - jnp.max(sh, axis=-1, keepdims=True)).astype(BF16))
        acc = _mm(jnp.concatenate(ps, axis=0), jnp.concatenate([r[0, g] for r in vw_refs], axis=0))
        one = ((g + 1) % 2) * NSA_HD
        owg = acc * (1.0 / acc[:, one:one + 1])
        for h in range(NSA_GROUP):
            o_sel[g][h] = osg[h * tq:(h + 1) * tq]
            o_win[g][h] = owg[h * tq:(h + 1) * tq]

    qg = queries(0)
    static_scores(0, qg)
    for g in range(NSA_KV):
        reset()

        def body(kt, carry, g=g, qg=qg):
            @pl.when(flag_ref[flag0 + g * nkt + kt] > 0)
            def _():
                tile = pl.ds(pl.multiple_of(kt * tks, tks), tks)
                _softmax_tile(_nt(qg, ks_ref[0, g, tile, :]), vs_ref[0, g, tile, :], None, tq, m_s, acc_s)
            return carry

        lax.fori_loop(0, last, body, 0)
        if g + 1 < NSA_KV:
            qg = queries(g + 1)
            static_scores(g + 1, qg)
        static_outputs(g)

    assemble = _assemble_heads
    gate = gate_ref[0]
    ghi = gate.astype(BF16)
    glo = (gate - ghi.astype(F32)).astype(BF16)
    ghl = jnp.concatenate([ghi, glo], axis=1)
    branches = (oc_ref[0], assemble(o_sel), assemble(o_win))
    o = jnp.zeros(branches[0].shape, F32)
    for c in range(3):
        o = o + _mm(ghl, gexp_ref[c]) * branches[c]
    y = _mm(o.astype(BF16), wout_ref[...])
    o_ref[0] = x_ref[0] + g1_ref[0] * y


def _attend(flags, q, ks, vs, kw, vw, selm, sf, oc, gate, gexp, x, g1, w_out):
    bsz, seq, d = x.shape
    G = NSA_KV
    tq, tks = _att_tiles(seq)
    nwt = WINDOW // tq
    tile = lambda w: pl.BlockSpec((1, tq, w), lambda b, i, f: (b, i, 0))
    kv = lambda w: pl.BlockSpec((1, G, seq, w), lambda b, i, f: (b, 0, 0, 0), pipeline_mode=pl.Buffered(1))
    wtile = lambda w, dj: pl.BlockSpec((1, G, tq, w), lambda b, i, f: (b, 0, jnp.maximum(i - dj, 0), 0))
    full = lambda a: pl.BlockSpec(a.shape, lambda b, i, f: tuple(0 for _ in a.shape))
    kw_specs = [wtile(KVW, dj) for dj in range(nwt, -1, -1)]
    vw_specs = [wtile(2 * NSA_HD, dj) for dj in range(nwt, -1, -1)]

    def body(flag_ref, q_ref, ks_ref, vs_ref, *rest):
        kw_refs = rest[:nwt + 1]
        vw_refs = rest[nwt + 1:2 * nwt + 2]
        _att_body(tks, flag_ref, q_ref, ks_ref, vs_ref, kw_refs, vw_refs, *rest[2 * nwt + 2:])

    return pl.pallas_call(
        body,
        grid_spec=pltpu.PrefetchScalarGridSpec(
            num_scalar_prefetch=1,
            grid=(bsz, seq // tq),
            in_specs=[tile(d), kv(KVW), kv(2 * NSA_HD), *kw_specs, *vw_specs, tile(KVW), full(sf), tile(d),
                      tile(128), full(gexp), tile(d), pl.BlockSpec((1, 1, d), lambda b, i, f: (b, 0, 0)),
                      full(w_out)],
            out_specs=tile(d),
            scratch_shapes=[pltpu.VMEM((NSA_GROUP * tq, 128), F32),
                            pltpu.VMEM((NSA_GROUP * tq, 2 * NSA_HD), F32),
                            pltpu.VMEM((2, NSA_GROUP * tq, tks), F32),
                            pltpu.VMEM((2, NSA_GROUP * tq, (nwt + 1) * tq), F32)]),
        out_shape=jax.ShapeDtypeStruct(x.shape, F32),
        compiler_params=_params(("arbitrary", "arbitrary")),
        name="nsa_attend",
    )(flags, q, ks, vs, *([kw] * (nwt + 1)), *([vw] * (nwt + 1)), selm, sf, oc, gate, gexp, x, g1, w_out)


def _nsa_layer(x, nw, sc, sh, g1, w_in, w_out, cmp_pe, cmp_w1, cmp_w2):
    bsz, seq, d = x.shape
    G, HPG, HD = NSA_KV, NSA_GROUP, NSA_HD
    wq = w_in[:, :d].reshape(d, G, HPG, HD).transpose(0, 2, 1, 3).reshape(d, d)
    ngate = 3 * NSA_HEADS
    wgate = jnp.pad(w_in[:, d + 6 * KVW:], ((0, 0), (0, 128 - ngate)))
    w_cat = jnp.concatenate([wq, w_in[:, d:d + 6 * KVW], wgate], axis=1).astype(BF16)
    w_out_p = w_out.reshape(G, HPG, HD, d).transpose(1, 0, 2, 3).reshape(d, d).astype(BF16)

    ns = seq // SEL_LEN
    assert ns <= HD and CMP_LEN == 2 * CMP_STRIDE
    pos = np.arange(seq)
    fks = np.zeros((G, seq, KVW), np.float32)
    fkw = np.zeros((G, seq, KVW), np.float32)
    fv = np.zeros((G, 1, 2 * HD), np.float32)
    sf = np.zeros((G * HPG, 1, KVW), np.float32)
    slopes = _alibi_slopes()
    for g in range(G):
        pick0, feat0 = ((g + 1) % G) * HD, ((g + 2) % G) * HD
        fks[g, pos, pick0 + pos // SEL_LEN] = 1.0
        for arr in (fks, fkw):
            arr[g, :, feat0 + 0] = arr[g, :, feat0 + 2] = (pos // 64) * 64
            arr[g, :, feat0 + 1] = arr[g, :, feat0 + 3] = pos % 64
        fv[g, 0, ((g + 1) % 2) * HD] = 1.0
        for h in range(HPG):
            s32 = np.float32(slopes[g][h])
            hi = s32.astype(BF16).astype(np.float32)
            lo = np.float32(s32 - hi).astype(BF16).astype(np.float32)
            sf[g * HPG + h, 0, feat0:feat0 + 4] = (hi, hi, lo, lo)
    q, kvc, ks, vs, kw, vw, gate = _nsa_in(x, nw, sc, sh, w_cat, jnp.asarray(fks, BF16),
                                           jnp.asarray(fkw, BF16), jnp.asarray(fv))

    nseg = seq // CMP_STRIDE

    def strides(t):
        return t.reshape(bsz, nseg, CMP_STRIDE, G, HD).transpose(0, 3, 1, 2, 4).reshape(
            bsz, G, nseg, CMP_STRIDE * HD)

    pe = cmp_pe.reshape(2, 2, CMP_STRIDE * HD)
    w1 = cmp_w1.reshape(2, 2, CMP_STRIDE * HD, HD).astype(BF16)
    kc, vc = _compress(strides(kvc[..., :KVW]), strides(kvc[..., KVW:]), pe, w1, cmp_w2.astype(BF16))
    mid = np.zeros((nseg, HD), np.float32)
    mid[:, 0] = mid[:, 2] = np.arange(nseg) * CMP_STRIDE
    mid[:, 1] = mid[:, 3] = (CMP_LEN - 1) / 2.0
    zero = jnp.zeros((bsz, nseg, HD), BF16)
    midb = jnp.broadcast_to(jnp.asarray(mid, BF16), (bsz, nseg, HD))
    kce, vce = [], []
    for g in range(G):
        slots = [zero] * G
        slots[g] = kc[:, g].astype(BF16)
        slots[(g + 2) % G] = midb
        kce.append(jnp.concatenate(slots, axis=-1))
        half = [zero, zero]
        half[g % 2] = vc[:, g].astype(BF16)
        vce.append(jnp.concatenate(half, axis=-1))
    kce = jnp.stack(kce, axis=1)
    vce = jnp.stack(vce, axis=1)

    ci = np.arange(nseg)[:, None] * CMP_STRIDE
    sj = np.arange(ns)[None, :] * SEL_LEN
    overlap = ((ci <= sj + SEL_LEN - 1) & (ci + CMP_LEN - 1 >= sj)
               & (np.arange(nseg)[:, None] < nseg - 1))
    ovt = jnp.asarray(overlap.T, F32)
    sf = jnp.asarray(sf)
    oc, selt, tile_any = _select(q, kce, vce, sf, ovt)
    selm = selt.transpose(0, 2, 1).astype(BF16)
    flags = (tile_any[..., 0] > 0.5).astype(jnp.int32).reshape(-1)

    gexp = np.zeros((3, 256, d), np.float32)
    for g in range(G):
        for h in range(HPG):
            for c in range(3):
                for half in range(2):
                    gexp[c, half * 128 + g * HPG * 3 + h * 3 + c, (h * G + g) * HD:(h * G + g + 1) * HD] = 1.0
    return _attend(flags, q, ks, vs, kw, vw, selm, sf, oc, gate, jnp.asarray(gexp, BF16), x, g1,
                   w_out_p)


def kernel(x, c, ada_w, ada_b, norm_mix, norm_ffn, final_norm, hg_w_in, hg_w_out, hg_gnorm, hg_lb,
           nsa_w_in, nsa_w_out, nsa_cmp_pe, nsa_cmp_w1, nsa_cmp_w2, ffn_w_up, ffn_conv_w, ffn_conv_b,
           ffn_w_down):
    depth = ada_w.shape[0]
    d = x.shape[-1]
    mod = _ada(c, ada_w, ada_b)
    for layer in range(depth):
        sh1, sc1, g1, sh2, sc2, g2 = [mod[layer, :, None, k * d:(k + 1) * d] for k in range(6)]
        j = layer // 2
        if layer % 2 == 0:
            x = _hgrn_layer(x, norm_mix[layer], sc1, sh1, g1, hg_w_in[j], hg_w_out[j], hg_gnorm[j], hg_lb, j)
        else:
            x = _nsa_layer(x, norm_mix[layer], sc1, sh1, g1, nsa_w_in[j], nsa_w_out[j], nsa_cmp_pe[j],
                           nsa_cmp_w1[j], nsa_cmp_w2[j])
        x = _ffn_layer(x, norm_ffn[layer], sc2, sh2, g2, ffn_w_up[layer], ffn_conv_w[layer],
                       ffn_conv_b[layer], ffn_w_down[layer], final_norm, layer == depth - 1)
    return x
```

```python
import functools

import numpy as np
import jax
import jax.numpy as jnp
from jax import lax
from jax.experimental import pallas as pl
from jax.experimental.pallas import tpu as pltpu

F32 = jnp.float32
BF16 = jnp.bfloat16
HIGHEST = lax.Precision.HIGHEST

EPS = 1e-6
NEG_INF = -1e30
LOG2_E = 1.4426950408889634
HG_HEADS = 8
HG_DK = 128
HG_CHUNK = 64
HG_SUB = 8
NSA_HEADS = 16
NSA_KV = 4
NSA_GROUP = NSA_HEADS // NSA_KV
NSA_HD = 64
CMP_LEN = 32
CMP_STRIDE = 16
SEL_LEN = 64
SEL_TOPK = 16
WINDOW = 512
FORCED_SCORE = 1e4
CONV_W = 3
KVW = NSA_KV * NSA_HD

VMEM_LIMIT_BYTES = 56 * 1024 * 1024
FFN_HALO = 16


def _mm(a, b):
    return jnp.dot(a, b, preferred_element_type=F32)


def _nt(a, b):
    return lax.dot_general(a, b, (((1,), (1,)), ((), ())), preferred_element_type=F32)


def _tn(a, b):
    return lax.dot_general(a, b, (((0,), (0,)), ((), ())), preferred_element_type=F32)


def _sigmoid(x):
    return 1.0 / (1.0 + jnp.exp(-x))


def _norm_mod(x, nw, sc, sh):
    ms = jnp.mean(x * x, axis=-1, keepdims=True)
    return (x * lax.rsqrt(ms + EPS) * nw) * (1.0 + sc) + sh


def _params(sem):
    return pltpu.CompilerParams(dimension_semantics=sem, vmem_limit_bytes=VMEM_LIMIT_BYTES)


def _ada_body(c_ref, w_ref, b_ref, o_ref):
    c = c_ref[...]
    ca = c * _sigmoid(c)
    o_ref[0] = jnp.dot(ca, w_ref[0], preferred_element_type=F32, precision=HIGHEST) + b_ref[0]


def _ada(c, ada_w, ada_b):
    depth, d, n6 = ada_w.shape
    bsz = c.shape[0]
    tn = n6 // 4
    return pl.pallas_call(
        _ada_body,
        grid=(depth, n6 // tn),
        in_specs=[pl.BlockSpec((bsz, d), lambda l, j: (0, 0)),
                  pl.BlockSpec((1, d, tn), lambda l, j: (l, 0, j)),
                  pl.BlockSpec((1, 1, tn), lambda l, j: (l, 0, j))],
        out_specs=pl.BlockSpec((1, bsz, tn), lambda l, j: (l, 0, j)),
        out_shape=jax.ShapeDtypeStruct((depth, bsz, n6), F32),
        compiler_params=_params(("arbitrary", "arbitrary")),
        name="ada_mod",
    )(c, ada_w, ada_b.reshape(depth, 1, n6))


def _pair_offset(j, i):
    nb = HG_CHUNK // HG_SUB
    before = sum(nb - 1 - jj for jj in range(j))
    return HG_SUB * (before + (i - j - 1))


def _hgrn_scale_stage(q_s, k_s, b2_s, b_s, r0, ls):
    C, SB = HG_CHUNK, HG_SUB
    nb = C // SB
    rows = pl.ds(r0, C)
    q = q_s[rows, ls]
    b = b2_s[rows, ls]
    b_s[0, :, ls] = b
    b_s[1, :, ls] = k_s[rows, ls]
    bq = [b[SB * i:SB * (i + 1)] for i in range(nb)]
    qq = [q[SB * i:SB * (i + 1)] for i in range(nb)]
    bend = [b_s[0, SB * j + SB - 1:SB * j + SB, ls] for j in range(nb)]
    lhs = []
    for j in range(nb - 1):
        for i in range(j + 1, nb):
            lhs.append(qq[i] * jnp.exp2(bq[i] - bend[j]))
    prod = []
    for i in range(nb):
        for s in range(SB):
            r = SB * i + s
            brow = b_s[0, r:r + 1, ls]
            krow = b_s[1, r:r + 1, ls]
            arg = bq[i] - brow
            if s > 0:
                arg = jnp.minimum(arg, 0.0)
            prod.append(qq[i] * krow * jnp.exp2(arg))
    return jnp.concatenate(lhs, axis=0).astype(BF16), jnp.concatenate(prod, axis=0).astype(BF16)


def _hgrn_score_stage(scaled, kt_s, r0, ls, ones_r, lane, lane_blk, causal):
    C, SB = HG_CHUNK, HG_SUB
    nb = C // SB
    lhs, prod = scaled
    res = _nt(lhs, kt_s[pl.ds(r0, C), ls])
    red = _mm(prod, ones_r)
    srows = []
    for i in range(nb):
        s_i = jnp.zeros((SB, C), F32)
        for j in range(i):
            off = _pair_offset(j, i)
            s_i = jnp.where(lane_blk == j, res[off:off + SB], s_i)
        for s in range(SB):
            r = SB * i + s
            s_i = jnp.where(lane == r, red[SB * r:SB * (r + 1)], s_i)
        srows.append(s_i)
    return jnp.where(causal, jnp.concatenate(srows, axis=0), 0.0).astype(BF16)


def _hgrn_output_stage(scores, v_s, qd_s, kd_s, dec_s, st_s, hidx, c, r0, ls):
    rows = pl.ds(r0, HG_CHUNK)
    v = v_s[rows, ls]
    st = st_s[hidx]
    o = _nt(qd_s[rows, ls], st.astype(BF16)) + _mm(scores, v)
    st_s[hidx] = st * dec_s[c, 0:1, ls] + _tn(v, kd_s[rows, ls])
    return o


def _hgrn_body(lb_row, x_ref, nw_ref, sc_ref, sh_ref, g1_ref, wq_ref, wf_ref, wi_ref, wg_ref, wout_ref,
               gn_ref, lb_ref, o_ref, h_s, q_s, k_s, b2_s, v_s, qd_s, kt_s, kd_s, dec_s, b_s, sc_s, oh_s, og_s,
               st_s):
    si = pl.program_id(1)
    hp = pl.program_id(2)
    ts = x_ref.shape[1]
    width = q_s.shape[1]
    hpb = width // HG_DK
    C, SB = HG_CHUNK, HG_SUB

    @pl.when(si == 0)
    def _():
        for hh in range(hpb):
            st_s[hp * hpb + hh] = jnp.zeros((HG_DK, HG_DK), F32)

    @pl.when(hp == 0)
    def _():
        h_s[...] = _norm_mod(x_ref[0], nw_ref[...], sc_ref[0], sh_ref[0]).astype(BF16)

    h = h_s[...]
    lbraw = lb_ref[...]
    e = jnp.exp(lbraw - jnp.max(lbraw, axis=0, keepdims=True))
    lb = jnp.sum(e[:lb_row + 1], axis=0, keepdims=True) / jnp.sum(e, axis=0, keepdims=True)
    q_s[...] = _mm(h, wq_ref[...])
    f = _mm(h, wf_ref[...])
    ef = jnp.exp(-jnp.abs(f))
    rcp = 1.0 / (1.0 + ef)
    pos = f >= 0
    sig_p = jnp.where(pos, rcp, ef * rcp)
    sig_n = jnp.where(pos, ef * rcp, rcp)
    lf2 = jnp.log(lb + (1.0 - lb) * sig_p) * LOG2_E
    k_s[...] = (1.0 - lb) * sig_n
    v_s[...] = _mm(h, wi_ref[...]).astype(BF16)

    ri = lax.broadcasted_iota(jnp.int32, (C, C), 0)
    ci = lax.broadcasted_iota(jnp.int32, (C, C), 1)
    causal = ri >= ci
    tril = jnp.where(causal, 1.0, 0.0).astype(BF16)
    hi = lf2.astype(BF16)
    r1 = lf2 - hi.astype(F32)
    mid = r1.astype(BF16)
    lo = (r1 - mid.astype(F32)).astype(BF16)
    pieces = jnp.concatenate([hi, mid, lo], axis=1)
    for c in range(ts // C):
        rows = slice(c * C, (c + 1) * C)
        cs = _mm(tril, pieces[rows])
        b = cs[:, :width] + cs[:, width:2 * width] + cs[:, 2 * width:]
        bend = jnp.concatenate([jnp.broadcast_to(b[j + SB - 1:j + SB], (SB, width)) for j in range(0, C, SB)],
                               axis=0)
        blast = jnp.broadcast_to(b[C - 1:C], (C, width))
        k = k_s[rows]
        b2_s[rows] = b
        qd_s[rows] = (q_s[rows] * jnp.exp2(b)).astype(BF16)
        kt_s[rows] = (k * jnp.exp2(bend - b)).astype(BF16)
        kd_s[rows] = (k * jnp.exp2(blast - b)).astype(BF16)
        dec_s[c] = jnp.exp2(blast[0:SB])

    ones_r = jnp.ones((HG_DK, C), BF16)
    lane = lax.broadcasted_iota(jnp.int32, (HG_SUB, C), 1)
    lane_blk = lane // HG_SUB

    lanes = lambda hh: slice(hh * HG_DK, (hh + 1) * HG_DK)

    def scores_of(c):
        r0 = c * C if isinstance(c, int) else pl.multiple_of(c * C, C)
        for hh in range(hpb):
            scaled = _hgrn_scale_stage(q_s, k_s, b2_s, b_s, r0, lanes(hh))
            sc_s[c % 2, hh] = _hgrn_score_stage(scaled, kt_s, r0, lanes(hh), ones_r, lane, lane_blk, causal)

    def outputs_of(c):
        r0 = c * C if isinstance(c, int) else pl.multiple_of(c * C, C)
        for hh in range(hpb):
            oh_s[pl.ds(r0, C), lanes(hh)] = _hgrn_output_stage(
                sc_s[c % 2, hh], v_s, qd_s, kd_s, dec_s, st_s, hp * hpb + hh, c, r0, lanes(hh))

    nchunk = ts // C
    scores_of(0)

    def chunk(c, carry):
        scores_of(c + 1)
        outputs_of(c)
        return carry

    lax.fori_loop(0, nchunk - 1, chunk, 0)
    outputs_of(nchunk - 1)

    g = _mm(h, wg_ref[...])
    gate = g * _sigmoid(g)
    gn = gn_ref[...]
    parts = []
    for hh in range(hpb):
        ls = slice(hh * HG_DK, (hh + 1) * HG_DK)
        oh = oh_s[:, ls]
        ms = jnp.mean(oh * oh, axis=-1, keepdims=True)
        parts.append(oh * lax.rsqrt(ms + EPS) * gn)
    og_s[hp] = (jnp.concatenate(parts, axis=1) * gate).astype(BF16)

    nblk = og_s.shape[0]

    @pl.when(hp == nblk - 1)
    def _():
        y = jnp.zeros(o_ref.shape[1:], F32)
        for blk in range(nblk):
            y = y + _mm(og_s[blk], wout_ref[blk * width:(blk + 1) * width, :])
        o_ref[0] = x_ref[0] + g1_ref[0] * y


def _hgrn_layer(x, nw, sc, sh, g1, w_in, w_out, gnorm, hg_lb, lb_row):
    bsz, seq, d = x.shape
    ts = min(512, seq)
    hpb = HG_HEADS
    width = hpb * HG_DK
    nblk = d // width
    w_in = w_in.astype(BF16)
    seg = lambda k: pl.BlockSpec((d, width), lambda b, s, hp, k=k: (0, k * nblk + hp))
    vec = pl.BlockSpec((1, 1, d), lambda b, s, hp: (b, 0, 0))
    return pl.pallas_call(
        functools.partial(_hgrn_body, lb_row),
        grid=(bsz, seq // ts, nblk),
        in_specs=[pl.BlockSpec((1, ts, d), lambda b, s, hp: (b, s, 0)),
                  pl.BlockSpec((1, d), lambda b, s, hp: (0, 0)),
                  vec, vec, vec,
                  seg(0), seg(1), seg(2), seg(3),
                  pl.BlockSpec((d, d), lambda b, s, hp: (0, 0)),
                  pl.BlockSpec((1, HG_DK), lambda b, s, hp: (0, 0)),
                  pl.BlockSpec((hg_lb.shape[0], width), lambda b, s, hp: (0, hp))],
        out_specs=pl.BlockSpec((1, ts, d), lambda b, s, hp: (b, s, 0)),
        out_shape=jax.ShapeDtypeStruct(x.shape, F32),
        scratch_shapes=[pltpu.VMEM((ts, d), BF16),
                        pltpu.VMEM((ts, width), F32), pltpu.VMEM((ts, width), F32),
                        pltpu.VMEM((ts, width), F32), pltpu.VMEM((ts, width), BF16),
                        pltpu.VMEM((ts, width), BF16), pltpu.VMEM((ts, width), BF16),
                        pltpu.VMEM((ts, width), BF16),
                        pltpu.VMEM((ts // HG_CHUNK, HG_SUB, width), F32),
                        pltpu.VMEM((2, HG_CHUNK, width), F32),
                        pltpu.VMEM((2, hpb, HG_CHUNK, HG_CHUNK), BF16),
                        pltpu.VMEM((ts, width), F32),
                        pltpu.VMEM((nblk, ts, width), BF16),
                        pltpu.VMEM((HG_HEADS, HG_DK, HG_DK), F32)],
        compiler_params=_params(("arbitrary", "arbitrary", "arbitrary")),
        name="hgrn2_layer",
    )(x, nw.reshape(1, d), sc, sh, g1, w_in, w_in, w_in, w_in, w_out.astype(BF16),
      gnorm.reshape(1, HG_DK), hg_lb)


def _ffn_body(final, nchunk, x_ref, xh_ref, nw_ref, sc_ref, sh_ref, g2_ref, wa_ref, wv_ref, cw_ref, cb_ref,
              wd_ref, fn_ref, o_ref, hx_s, a_s):
    i = pl.program_id(1)
    tm = x_ref.shape[1]
    H = FFN_HALO
    x = x_ref[0]
    nw, sc, sh = nw_ref[...], sc_ref[0], sh_ref[0]
    hx_s[H:, :] = _norm_mod(x, nw, sc, sh).astype(BF16)
    hx_s[0:H, :] = jnp.where(i > 0, _norm_mod(xh_ref[0], nw, sc, sh), 0.0).astype(BF16)
    ff = wa_ref.shape[1]
    fc = ff // nchunk
    y = jnp.zeros((tm, x.shape[1]), F32)
    for c in range(nchunk):
        cols = slice(c * fc, (c + 1) * fc)
        a_s[...] = _mm(hx_s[...], wa_ref[:, cols])
        v = _mm(hx_s[H:, :], wv_ref[:, cols])
        cw = cw_ref[:, cols]
        conv = (cw[0:1] * a_s[H - 2:H - 2 + tm, :] + cw[1:2] * a_s[H - 1:H - 1 + tm, :]
                + cw[2:3] * a_s[H:H + tm, :] + cb_ref[:, cols])
        u = (conv * _sigmoid(conv) * v).astype(BF16)
        y = y + _mm(u, wd_ref[cols, :])
    out = x + g2_ref[0] * y
    if final:
        ms = jnp.mean(out * out, axis=-1, keepdims=True)
        out = out * lax.rsqrt(ms + EPS) * fn_ref[...]
    o_ref[0] = out


def _ffn_layer(x, nw, sc, sh, g2, w_up, conv_w, conv_b, w_down, final_norm, final):
    bsz, seq, d = x.shape
    ff = w_down.shape[0]
    tm = min(512, seq)
    nchunk = 1
    H = FFN_HALO
    wa = w_up[:, :ff].astype(BF16)
    wv = w_up[:, ff:].astype(BF16)
    vec = pl.BlockSpec((1, 1, d), lambda b, i: (b, 0, 0))
    full = lambda shape: pl.BlockSpec(shape, lambda b, i: tuple(0 for _ in shape),
                                      pipeline_mode=pl.Buffered(1))
    return pl.pallas_call(
        functools.partial(_ffn_body, final, nchunk),
        grid=(bsz, seq // tm),
        in_specs=[pl.BlockSpec((1, tm, d), lambda b, i: (b, i, 0)),
                  pl.BlockSpec((1, H, d), lambda b, i: (b, jnp.maximum(i * (tm // H) - 1, 0), 0)),
                  full((1, d)), vec, vec, vec,
                  full((d, ff)), full((d, ff)), full((CONV_W, ff)), full((1, ff)), full((ff, d)),
                  full((1, d))],
        out_specs=pl.BlockSpec((1, tm, d), lambda b, i: (b, i, 0)),
        out_shape=jax.ShapeDtypeStruct(x.shape, F32),
        scratch_shapes=[pltpu.VMEM((tm + H, d), BF16), pltpu.VMEM((tm + H, ff // nchunk), F32)],
        compiler_params=_params(("arbitrary", "arbitrary")),
        name="conv_ffn_layer",
    )(x, x, nw.reshape(1, d), sc, sh, g2, wa, wv, conv_w, conv_b.reshape(1, ff), w_down.astype(BF16),
      final_norm.reshape(1, d))


def _alibi_slopes():
    return [[2.0 ** (-8.0 * (g * NSA_GROUP + h + 1) / NSA_HEADS) for h in range(NSA_GROUP)]
            for g in range(NSA_KV)]


def _slot(lane, g):
    return (lane >= g * NSA_HD) & (lane < (g + 1) * NSA_HD)


def _nsa_in_body(x_ref, nw_ref, sc_ref, sh_ref, w_ref, fks_ref, fkw_ref, fv_ref,
                 q_ref, kvc_ref, ks_ref, vs_ref, kw_ref, vw_ref, gate_ref):
    d = x_ref.shape[2]
    h = _norm_mod(x_ref[0], nw_ref[...], sc_ref[0], sh_ref[0]).astype(BF16)
    p = _mm(h, w_ref[...])
    q_ref[0] = (p[:, :d] * (NSA_HD ** -0.5)).astype(BF16)
    o = d
    for i in range(kvc_ref.shape[1]):
        kvc_ref[0, i] = p[:, o + i * 2 * NSA_HD:o + (i + 1) * 2 * NSA_HD]
    o += 2 * KVW
    ksb, vsb, kwb, vwb = [p[:, o + i * KVW:o + (i + 1) * KVW].astype(BF16) for i in range(4)]
    o += 4 * KVW
    lane = lax.broadcasted_iota(jnp.int32, (1, KVW), 1)
    lane_v = lax.broadcasted_iota(jnp.int32, (1, 2 * NSA_HD), 1)
    for g in range(NSA_KV):
        own = _slot(lane, g)
        ks_ref[0, g] = jnp.where(own, ksb, fks_ref[g])
        kw_ref[0, g] = jnp.where(own, kwb, fkw_ref[g])
        half = slice((g // 2) * 2 * NSA_HD, (g // 2 + 1) * 2 * NSA_HD)
        own_v = _slot(lane_v, g % 2)
        fv = fv_ref[g].astype(BF16)
        vs_ref[0, g] = jnp.where(own_v, vsb[:, half], fv)
        vw_ref[0, g] = jnp.where(own_v, vwb[:, half], fv)
    gate_ref[0] = _sigmoid(p[:, o:])


def _nsa_in(x, nw, sc, sh, w_cat, fks, fkw, fv):
    bsz, seq, d = x.shape
    tm = min(512, seq)
    ncol = w_cat.shape[1]
    G = NSA_KV
    tile = lambda w: pl.BlockSpec((1, tm, w), lambda b, i: (b, i, 0))
    gtile = lambda w: pl.BlockSpec((1, G, tm, w), lambda b, i: (b, 0, i, 0))
    vec = pl.BlockSpec((1, 1, d), lambda b, i: (b, 0, 0))
    sds = lambda w, dt: jax.ShapeDtypeStruct((bsz, seq, w), dt)
    gsds = lambda w: jax.ShapeDtypeStruct((bsz, G, seq, w), BF16)
    feat = pl.BlockSpec((G, tm, KVW), lambda b, i: (0, i, 0))
    return pl.pallas_call(
        _nsa_in_body,
        grid=(bsz, seq // tm),
        in_specs=[tile(d), pl.BlockSpec((1, d), lambda b, i: (0, 0)), vec, vec,
                  pl.BlockSpec((d, ncol), lambda b, i: (0, 0)), feat, feat,
                  pl.BlockSpec((G, 1, 2 * NSA_HD), lambda b, i: (0, 0, 0))],
        out_specs=[tile(d), gtile(2 * NSA_HD), gtile(KVW), gtile(2 * NSA_HD), gtile(KVW), gtile(2 * NSA_HD),
                   tile(128)],
        out_shape=[sds(d, BF16), jax.ShapeDtypeStruct((bsz, G, seq, 2 * NSA_HD), F32), gsds(KVW),
                   gsds(2 * NSA_HD), gsds(KVW), gsds(2 * NSA_HD), sds(128, F32)],
        compiler_params=_params(("arbitrary", "arbitrary")),
        name="nsa_in_proj",
    )(x, nw.reshape(1, d), sc, sh, w_cat, fks, fkw, fv)


def _cmp_body(kvc_ref, pe_ref, w1_ref, w2_ref, mid_ref, kc_ref, vc_ref):
    nseg = kc_ref.shape[2]
    PW = 2 * NSA_HD
    row = lax.broadcasted_iota(jnp.int32, (nseg, PW), 0)
    lane = lax.broadcasted_iota(jnp.int32, (1, KVW), 1)
    lane_v = lax.broadcasted_iota(jnp.int32, (1, PW), 1)
    outs = []
    for i in range(2):
        pairs = []
        for pair in range(NSA_KV // 2):
            x = jnp.concatenate([kvc_ref[0, 2 * i + pair, pl.ds(l, nseg, stride=CMP_STRIDE), :]
                                 for l in range(CMP_STRIDE)], axis=1)
            a = _mm((x + pe_ref[i, 0]).astype(BF16), w1_ref[i, 0])
            bm = _mm((x + pe_ref[i, 1]).astype(BF16), w1_ref[i, 1])
            pre = a + pltpu.roll(bm, nseg - 1, 0)
            act = pre * _sigmoid(pre)
            out = _mm(act.astype(BF16), w2_ref[i])
            pairs.append(jnp.where(row < nseg - 1, out, 0.0))
        outs.append(pairs)
    kc_all = jnp.concatenate(outs[0], axis=1).astype(BF16)
    for g in range(NSA_KV):
        kc_ref[0, g] = jnp.where(_slot(lane, g), kc_all, mid_ref[g])
        vc_ref[0, g] = jnp.where(_slot(lane_v, g % 2), outs[1][g // 2], 0.0).astype(BF16)


def _compress(kvc, pe, w1, w2, mid):
    bsz, nwin, seq, width = kvc.shape
    nseg = seq // CMP_STRIDE
    G = NSA_KV
    full = lambda a: pl.BlockSpec(a.shape, lambda b: tuple(0 for _ in a.shape))
    return pl.pallas_call(
        _cmp_body,
        grid=(bsz,),
        in_specs=[pl.BlockSpec((1, nwin, seq, width), lambda b: (b, 0, 0, 0)), full(pe), full(w1), full(w2),
                  full(mid)],
        out_specs=[pl.BlockSpec((1, G, nseg, KVW), lambda b: (b, 0, 0, 0)),
                   pl.BlockSpec((1, G, nseg, 2 * NSA_HD), lambda b: (b, 0, 0, 0))],
        out_shape=[jax.ShapeDtypeStruct((bsz, G, nseg, KVW), BF16),
                   jax.ShapeDtypeStruct((bsz, G, nseg, 2 * NSA_HD), BF16)],
        compiler_params=_params(("arbitrary",)),
        name="nsa_compress",
    )(kvc, pe, w1, w2, mid)


def _assemble_heads(parts):
    tq = parts[0][0].shape[0]
    first = lax.broadcasted_iota(jnp.int32, (1, 2 * NSA_HD), 1) < NSA_HD
    cols = []
    for h in range(NSA_GROUP):
        for pair in range(NSA_KV // 2):
            cols.append(jnp.where(first, parts[2 * pair][h], parts[2 * pair + 1][h]))
    return jnp.concatenate(cols, axis=1)


def _sel_body(nsel, q_ref, kc_ref, vc_ref, sf_ref, ovt_ref, oc_ref, selm_ref, flag_ref):
    qi = pl.program_id(1)
    tq = q_ref.shape[1]
    n = kc_ref.shape[2]
    ns = ovt_ref.shape[0]
    q0 = qi * tq
    SB = 8
    tcol = q0 + lax.broadcasted_iota(jnp.int32, (tq, 1), 0)
    ncol = lax.broadcasted_iota(jnp.int32, (1, n), 1)
    valid = (ncol * CMP_STRIDE + (CMP_LEN - 1)) <= tcol
    rowok = jnp.where(tcol >= CMP_LEN - 1, 1.0, 0.0)
    lane = lax.broadcasted_iota(jnp.int32, (1, KVW), 1)
    jidx = lax.broadcasted_iota(jnp.int32, (ns, tq), 0)
    jloc = lax.broadcasted_iota(jnp.int32, (SB, tq), 0)
    tl = q0 + lax.broadcasted_iota(jnp.int32, (ns, tq), 1)
    cur = tl // SEL_LEN
    valid_s = jidx * SEL_LEN <= tl
    forced = ((jidx == 0) | (jidx == cur) | (jidx == cur - 1)) & valid_s
    parts = [[None] * NSA_GROUP for _ in range(NSA_KV)]
    mask_rows = [None] * NSA_KV
    for g in range(NSA_KV):
        own = _slot(lane, g)
        psum = jnp.zeros((tq, n), F32)
        ps = []
        for h in range(NSA_GROUP):
            qe = jnp.where(own, q_ref[0, :, h * KVW:(h + 1) * KVW], sf_ref[g * NSA_GROUP + h].astype(BF16))
            s = jnp.where(valid, _nt(qe, kc_ref[0, g]), NEG_INF)
            ex = jnp.exp(s - jnp.max(s, axis=-1, keepdims=True))
            p = ex * (rowok / jnp.sum(ex, axis=-1, keepdims=True))
            psum = psum + p
            ps.append(p.astype(BF16))
        og = _mm(jnp.concatenate(ps, axis=0), vc_ref[0, g])
        for h in range(NSA_GROUP):
            parts[g][h] = og[h * tq:(h + 1) * tq]
        imp = lax.dot_general(ovt_ref[...], psum, (((1,), (1,)), ((), ())),
                              preferred_element_type=F32, precision=HIGHEST)
        score = jnp.where(forced, FORCED_SCORE, jnp.where(valid_s, imp, -1.0))
        blocks = [score[SB * k:SB * (k + 1)] for k in range(ns // SB)]
        cnts = [jnp.zeros((SB, tq), F32) for _ in blocks]
        for jp in range(ns):
            row = score[jp:jp + 1, :]
            for k, blk in enumerate(blocks):
                if SB * k > jp:
                    beats = jnp.where(row >= blk, 1.0, 0.0)
                elif SB * (k + 1) <= jp:
                    beats = jnp.where(row > blk, 1.0, 0.0)
                else:
                    beats = jnp.where(jloc > jp - SB * k, jnp.where(row >= blk, 1.0, 0.0),
                                      jnp.where(row > blk, 1.0, 0.0))
                cnts[k] = cnts[k] + beats
        cnt = jnp.concatenate(cnts, axis=0)
        rows = jnp.where(cnt < nsel, 0.0, NEG_INF)
        if ns < NSA_HD:
            rows = jnp.concatenate([rows, jnp.zeros((NSA_HD - ns, tq), F32)], axis=0)
        mask_rows[(g + 1) % NSA_KV] = rows
        anyq = jnp.max(jnp.where(cnt < nsel, 1.0, 0.0), axis=1, keepdims=True)
        bpt = ns // flag_ref.shape[3]
        flag_ref[0, 0, g] = jnp.concatenate(
            [jnp.broadcast_to(jnp.max(anyq[bpt * kt:bpt * (kt + 1)], axis=0, keepdims=True), (1, 128))
             for kt in range(flag_ref.shape[3])], axis=0)
    selm_ref[0] = jnp.concatenate(mask_rows, axis=0).T.astype(BF16)
    oc_ref[0] = _assemble_heads(parts)


def _att_tiles(seq):
    return min(256, seq), min(512, seq)


def _select(q, kc, vc, sf, ovt):
    bsz, seq, d = q.shape
    n = kc.shape[2]
    ns = ovt.shape[0]
    tq, tks = _att_tiles(seq)
    nkt = seq // tks
    nsel = min(SEL_TOPK, ns)
    assert ns % 8 == 0 and ns % nkt == 0
    return pl.pallas_call(
        functools.partial(_sel_body, nsel),
        grid=(bsz, seq // tq),
        in_specs=[pl.BlockSpec((1, tq, d), lambda b, i: (b, i, 0)),
                  pl.BlockSpec((1, NSA_KV, n, KVW), lambda b, i: (b, 0, 0, 0)),
                  pl.BlockSpec((1, NSA_KV, n, 2 * NSA_HD), lambda b, i: (b, 0, 0, 0)),
                  pl.BlockSpec(sf.shape, lambda b, i: (0, 0, 0)),
                  pl.BlockSpec((ns, n), lambda b, i: (0, 0))],
        out_specs=[pl.BlockSpec((1, tq, d), lambda b, i: (b, i, 0)),
                   pl.BlockSpec((1, tq, KVW), lambda b, i: (b, i, 0)),
                   pl.BlockSpec((1, 1, NSA_KV, nkt, 128), lambda b, i: (b, i, 0, 0, 0))],
        out_shape=[jax.ShapeDtypeStruct((bsz, seq, d), F32),
                   jax.ShapeDtypeStruct((bsz, seq, KVW), BF16),
                   jax.ShapeDtypeStruct((bsz, seq // tq, NSA_KV, nkt, 128), F32)],
        compiler_params=_params(("arbitrary", "arbitrary")),
        name="nsa_compressed_select",
    )(q, kc, vc, sf, ovt)


def _softmax_tile(s, vb, mask, tq, m_s, acc_s):
    tk = vb.shape[0]
    ps = []
    for h in range(NSA_GROUP):
        rows = slice(h * tq, (h + 1) * tq)
        sh = s[rows]
        if mask is not None:
            sh = jnp.where(mask, sh, NEG_INF)
        m_old = m_s[rows]
        m_new = jnp.maximum(m_old, jnp.max(sh, axis=-1, keepdims=True))
        p = jnp.exp(sh - jnp.concatenate([m_new] * (tk // 128), axis=1))
        m_s[rows] = m_new
        acc_s[rows] = acc_s[rows] * jnp.exp(m_old - m_new)
        ps.append(p.astype(BF16))
    acc_s[...] += _mm(jnp.concatenate(ps, axis=0), vb)


def _finish(g, acc_s):
    one = ((g + 1) % 2) * NSA_HD
    acc = acc_s[...]
    return acc * (1.0 / acc[:, one:one + 1])


def _att_body(tks, flag_ref, q_ref, ks_ref, vs_ref, kw_refs, vw_refs, selm_ref, sf_ref, oc_ref, gate_ref,
              gexp_ref, x_ref, g1_ref, wout_ref, o_ref, m_s, acc_s, sd_s, sw_s):
    qi = pl.program_id(1)
    tq = q_ref.shape[1]
    q0 = qi * tq
    nkt = ks_ref.shape[2] // tks
    flag0 = (pl.program_id(0) * pl.num_programs(1) + qi) * (NSA_KV * nkt)
    nwt = len(kw_refs) - 1
    lane = lax.broadcasted_iota(jnp.int32, (1, KVW), 1)
    lane_v = lax.broadcasted_iota(jnp.int32, (1, 2 * NSA_HD), 1)
    ql = lax.broadcasted_iota(jnp.int32, (tq, 1), 0)
    colw = lax.broadcasted_iota(jnp.int32, (1, (nwt + 1) * tq), 1)
    rel = jnp.where(colw // tq + qi >= nwt, colw - nwt * tq - ql, 1)
    wmask = (rel <= 0) & (rel > -WINDOW)
    selm = selm_ref[0]
    o_sel = [[None] * NSA_GROUP for _ in range(NSA_KV)]
    o_win = [[None] * NSA_GROUP for _ in range(NSA_KV)]

    def reset():
        m_s[...] = jnp.full(m_s.shape, NEG_INF, F32)
        acc_s[...] = jnp.zeros(acc_s.shape, F32)

    last = (q0 + tq + tks - 1) // tks - 1
    diag = pl.ds(pl.multiple_of(last * tks, tks), tks)
    dmask = (lax.broadcasted_iota(jnp.int32, (1, tks), 1) + last * tks) <= (ql + q0)

    def queries(g):
        own = _slot(lane, g)
        pick = _slot(lane, (g + 1) % NSA_KV)
        return jnp.concatenate(
            [jnp.where(own, q_ref[0, :, h * KVW:(h + 1) * KVW],
                       jnp.where(pick, selm, sf_ref[g * NSA_GROUP + h].astype(BF16)))
             for h in range(NSA_GROUP)], axis=0)

    def static_scores(g, qg):
        sd_s[g % 2] = _nt(qg, ks_ref[0, g, diag, :])
        sw_s[g % 2] = _nt(qg, jnp.concatenate([r[0, g] for r in kw_refs], axis=0))

    def static_outputs(g):
        _softmax_tile(sd_s[g % 2], vs_ref[0, g, diag, :], dmask, tq, m_s, acc_s)
        osg = _finish(g, acc_s)
        ps = []
        for h in range(NSA_GROUP):
            sh = jnp.where(wmask, sw_s[g % 2, h * tq:(h + 1) * tq], NEG_INF)
            ps.append(jnp.exp(sh - jnp.max(sh, axis=-1, keepdims=True)).astype(BF16))
        acc = _mm(jnp.concatenate(ps, axis=0), jnp.concatenate([r[0, g] for r in vw_refs], axis=0))
        one = ((g + 1) % 2) * NSA_HD
        owg = acc * (1.0 / acc[:, one:one + 1])
        for h in range(NSA_GROUP):
            o_sel[g][h] = osg[h * tq:(h + 1) * tq]
            o_win[g][h] = owg[h * tq:(h + 1) * tq]

    qg = queries(0)
    static_scores(0, qg)
    for g in range(NSA_KV):
        reset()

        def body(kt, carry, g=g, qg=qg):
            @pl.when(flag_ref[flag0 + g * nkt + kt] > 0)
            def _():
                tile = pl.ds(pl.multiple_of(kt * tks, tks), tks)
                _softmax_tile(_nt(qg, ks_ref[0, g, tile, :]), vs_ref[0, g, tile, :], None, tq, m_s, acc_s)
            return carry

        lax.fori_loop(0, last, body, 0)
        if g + 1 < NSA_KV:
            qg = queries(g + 1)
            static_scores(g + 1, qg)
        static_outputs(g)

    assemble = _assemble_heads
    gate = gate_ref[0]
    ghi = gate.astype(BF16)
    glo = (gate - ghi.astype(F32)).astype(BF16)
    ghl = jnp.concatenate([ghi, glo], axis=1)
    branches = (oc_ref[0], assemble(o_sel), assemble(o_win))
    o = jnp.zeros(branches[0].shape, F32)
    for c in range(3):
        o = o + _mm(ghl, gexp_ref[c]) * branches[c]
    y = _mm(o.astype(BF16), wout_ref[...])
    o_ref[0] = x_ref[0] + g1_ref[0] * y


def _attend(flags, q, ks, vs, kw, vw, selm, sf, oc, gate, gexp, x, g1, w_out):
    bsz, seq, d = x.shape
    G = NSA_KV
    tq, tks = _att_tiles(seq)
    nwt = WINDOW // tq
    tile = lambda w: pl.BlockSpec((1, tq, w), lambda b, i, f: (b, i, 0))
    kv = lambda w: pl.BlockSpec((1, G, seq, w), lambda b, i, f: (b, 0, 0, 0), pipeline_mode=pl.Buffered(1))
    wtile = lambda w, dj: pl.BlockSpec((1, G, tq, w), lambda b, i, f: (b, 0, jnp.maximum(i - dj, 0), 0))
    full = lambda a: pl.BlockSpec(a.shape, lambda b, i, f: tuple(0 for _ in a.shape))
    kw_specs = [wtile(KVW, dj) for dj in range(nwt, -1, -1)]
    vw_specs = [wtile(2 * NSA_HD, dj) for dj in range(nwt, -1, -1)]

    def body(flag_ref, q_ref, ks_ref, vs_ref, *rest):
        kw_refs = rest[:nwt + 1]
        vw_refs = rest[nwt + 1:2 * nwt + 2]
        _att_body(tks, flag_ref, q_ref, ks_ref, vs_ref, kw_refs, vw_refs, *rest[2 * nwt + 2:])

    return pl.pallas_call(
        body,
        grid_spec=pltpu.PrefetchScalarGridSpec(
            num_scalar_prefetch=1,
            grid=(bsz, seq // tq),
            in_specs=[tile(d), kv(KVW), kv(2 * NSA_HD), *kw_specs, *vw_specs, tile(KVW), full(sf), tile(d),
                      tile(128), full(gexp), tile(d), pl.BlockSpec((1, 1, d), lambda b, i, f: (b, 0, 0)),
                      full(w_out)],
            out_specs=tile(d),
            scratch_shapes=[pltpu.VMEM((NSA_GROUP * tq, 128), F32),
                            pltpu.VMEM((NSA_GROUP * tq, 2 * NSA_HD), F32),
                            pltpu.VMEM((2, NSA_GROUP * tq, tks), F32),
                            pltpu.VMEM((2, NSA_GROUP * tq, (nwt + 1) * tq), F32)]),
        out_shape=jax.ShapeDtypeStruct(x.shape, F32),
        compiler_params=_params(("arbitrary", "arbitrary")),
        name="nsa_attend",
    )(flags, q, ks, vs, *([kw] * (nwt + 1)), *([vw] * (nwt + 1)), selm, sf, oc, gate, gexp, x, g1, w_out)


def _nsa_layer(x, nw, sc, sh, g1, w_in, w_out, cmp_pe, cmp_w1, cmp_w2):
    bsz, seq, d = x.shape
    G, HPG, HD = NSA_KV, NSA_GROUP, NSA_HD
    wq = w_in[:, :d].reshape(d, G, HPG, HD).transpose(0, 2, 1, 3).reshape(d, d)
    ngate = 3 * NSA_HEADS
    wgate = jnp.pad(w_in[:, d + 6 * KVW:], ((0, 0), (0, 128 - ngate)))
    w_cat = jnp.concatenate([wq, w_in[:, d:d + 6 * KVW], wgate], axis=1).astype(BF16)
    w_out_p = w_out.reshape(G, HPG, HD, d).transpose(1, 0, 2, 3).reshape(d, d).astype(BF16)

    ns = seq // SEL_LEN
    assert ns <= HD and CMP_LEN == 2 * CMP_STRIDE
    pos = np.arange(seq)
    fks = np.zeros((G, seq, KVW), np.float32)
    fkw = np.zeros((G, seq, KVW), np.float32)
    fv = np.zeros((G, 1, 2 * HD), np.float32)
    sf = np.zeros((G * HPG, 1, KVW), np.float32)
    slopes = _alibi_slopes()
    for g in range(G):
        pick0, feat0 = ((g + 1) % G) * HD, ((g + 2) % G) * HD
        fks[g, pos, pick0 + pos // SEL_LEN] = 1.0
        for arr in (fks, fkw):
            arr[g, :, feat0 + 0] = arr[g, :, feat0 + 2] = (pos // 64) * 64
            arr[g, :, feat0 + 1] = arr[g, :, feat0 + 3] = pos % 64
        fv[g, 0, ((g + 1) % 2) * HD] = 1.0
        for h in range(HPG):
            s32 = np.float32(slopes[g][h])
            hi = s32.astype(BF16).astype(np.float32)
            lo = np.float32(s32 - hi).astype(BF16).astype(np.float32)
            sf[g * HPG + h, 0, feat0:feat0 + 4] = (hi, hi, lo, lo)
    q, kvc, ks, vs, kw, vw, gate = _nsa_in(x, nw, sc, sh, w_cat, jnp.asarray(fks, BF16),
                                           jnp.asarray(fkw, BF16), jnp.asarray(fv))

    nseg = seq // CMP_STRIDE
    eye2 = jnp.eye(2, dtype=F32)
    w1 = jnp.einsum("khlde,pq->khlpdqe", cmp_w1.reshape(2, 2, CMP_STRIDE, HD, HD), eye2).reshape(
        2, 2, CMP_STRIDE * 2 * HD, 2 * HD).astype(BF16)
    pe = jnp.broadcast_to(cmp_pe.reshape(2, 2, CMP_STRIDE, 1, HD), (2, 2, CMP_STRIDE, 2, HD)).reshape(
        2, 2, 1, CMP_STRIDE * 2 * HD)
    w2 = jnp.einsum("kde,pq->kpdqe", cmp_w2, eye2).reshape(2, 2 * HD, 2 * HD).astype(BF16)
    mid = np.zeros((G, nseg, KVW), np.float32)
    for g in range(G):
        feat0 = ((g + 2) % G) * HD
        mid[g, :, feat0 + 0] = mid[g, :, feat0 + 2] = np.arange(nseg) * CMP_STRIDE
        mid[g, :, feat0 + 1] = mid[g, :, feat0 + 3] = (CMP_LEN - 1) / 2.0
    kce, vce = _compress(kvc, pe, w1, w2, jnp.asarray(mid, BF16))

    ci = np.arange(nseg)[:, None] * CMP_STRIDE
    sj = np.arange(ns)[None, :] * SEL_LEN
    overlap = ((ci <= sj + SEL_LEN - 1) & (ci + CMP_LEN - 1 >= sj)
               & (np.arange(nseg)[:, None] < nseg - 1))
    ovt = jnp.asarray(overlap.T, F32)
    sf = jnp.asarray(sf)
    oc, selm, tile_any = _select(q, kce, vce, sf, ovt)
    flags = (tile_any[..., 0] > 0.5).astype(jnp.int32).reshape(-1)

    gexp = np.zeros((3, 256, d), np.float32)
    for g in range(G):
        for h in range(HPG):
            for c in range(3):
                for half in range(2):
                    gexp[c, half * 128 + g * HPG * 3 + h * 3 + c, (h * G + g) * HD:(h * G + g + 1) * HD] = 1.0
    return _attend(flags, q, ks, vs, kw, vw, selm, sf, oc, gate, jnp.asarray(gexp, BF16), x, g1,
                   w_out_p)


def kernel(x, c, ada_w, ada_b, norm_mix, norm_ffn, final_norm, hg_w_in, hg_w_out, hg_gnorm, hg_lb,
           nsa_w_in, nsa_w_out, nsa_cmp_pe, nsa_cmp_w1, nsa_cmp_w2, ffn_w_up, ffn_conv_w, ffn_conv_b,
           ffn_w_down):
    depth = ada_w.shape[0]
    d = x.shape[-1]
    mod = _ada(c, ada_w, ada_b)
    for layer in range(depth):
        sh1, sc1, g1, sh2, sc2, g2 = [mod[layer, :, None, k * d:(k + 1) * d] for k in range(6)]
        j = layer // 2
        if layer % 2 == 0:
            x = _hgrn_layer(x, norm_mix[layer], sc1, sh1, g1, hg_w_in[j], hg_w_out[j], hg_gnorm[j], hg_lb, j)
        else:
            x = _nsa_layer(x, norm_mix[layer], sc1, sh1, g1, nsa_w_in[j], nsa_w_out[j], nsa_cmp_pe[j],
                           nsa_cmp_w1[j], nsa_cmp_w2[j])
        x = _ffn_layer(x, norm_ffn[layer], sc2, sh2, g2, ffn_w_up[layer], ffn_conv_w[layer],
                       ffn_conv_b[layer], ffn_w_down[layer], final_norm, layer == depth - 1)
    return x
```

```python
import functools

import numpy as np
import jax
import jax.numpy as jnp
from jax import lax
from jax.experimental import pallas as pl
from jax.experimental.pallas import tpu as pltpu

F32 = jnp.float32
BF16 = jnp.bfloat16
HIGHEST = lax.Precision.HIGHEST

EPS = 1e-6
NEG_INF = -1e30
LOG2_E = 1.4426950408889634
HG_HEADS = 8
HG_DK = 128
HG_CHUNK = 64
HG_SUB = 8
NSA_HEADS = 16
NSA_KV = 4
NSA_GROUP = NSA_HEADS // NSA_KV
NSA_HD = 64
CMP_LEN = 32
CMP_STRIDE = 16
SEL_LEN = 64
SEL_TOPK = 16
WINDOW = 512
FORCED_SCORE = 1e4
CONV_W = 3
KVW = NSA_KV * NSA_HD

VMEM_LIMIT_BYTES = 56 * 1024 * 1024
FFN_HALO = 16


def _mm(a, b):
    return jnp.dot(a, b, preferred_element_type=F32)


def _nt(a, b):
    return lax.dot_general(a, b, (((1,), (1,)), ((), ())), preferred_element_type=F32)


def _tn(a, b):
    return lax.dot_general(a, b, (((0,), (0,)), ((), ())), preferred_element_type=F32)


def _sigmoid(x):
    return 1.0 / (1.0 + jnp.exp(-x))


def _norm_mod(x, nw, sc, sh):
    ms = jnp.mean(x * x, axis=-1, keepdims=True)
    return (x * lax.rsqrt(ms + EPS) * nw) * (1.0 + sc) + sh


def _params(sem):
    return pltpu.CompilerParams(dimension_semantics=sem, vmem_limit_bytes=VMEM_LIMIT_BYTES)


def _ada_body(c_ref, w_ref, b_ref, o_ref):
    c = c_ref[...]
    ca = c * _sigmoid(c)
    o_ref[0] = jnp.dot(ca, w_ref[0], preferred_element_type=F32, precision=HIGHEST) + b_ref[0]


def _ada(c, ada_w, ada_b):
    depth, d, n6 = ada_w.shape
    bsz = c.shape[0]
    tn = n6 // 4
    return pl.pallas_call(
        _ada_body,
        grid=(depth, n6 // tn),
        in_specs=[pl.BlockSpec((bsz, d), lambda l, j: (0, 0)),
                  pl.BlockSpec((1, d, tn), lambda l, j: (l, 0, j)),
                  pl.BlockSpec((1, 1, tn), lambda l, j: (l, 0, j))],
        out_specs=pl.BlockSpec((1, bsz, tn), lambda l, j: (l, 0, j)),
        out_shape=jax.ShapeDtypeStruct((depth, bsz, n6), F32),
        compiler_params=_params(("arbitrary", "arbitrary")),
        name="ada_mod",
    )(c, ada_w, ada_b.reshape(depth, 1, n6))


def _pair_offset(j, i):
    nb = HG_CHUNK // HG_SUB
    before = sum(nb - 1 - jj for jj in range(j))
    return HG_SUB * (before + (i - j - 1))


def _hgrn_scale_stage(q_s, k_s, b2_s, b_s, r0, ls):
    C, SB = HG_CHUNK, HG_SUB
    nb = C // SB
    rows = pl.ds(r0, C)
    q = q_s[rows, ls]
    b = b2_s[rows, ls]
    b_s[0, :, ls] = b
    b_s[1, :, ls] = k_s[rows, ls]
    bq = [b[SB * i:SB * (i + 1)] for i in range(nb)]
    qq = [q[SB * i:SB * (i + 1)] for i in range(nb)]
    bend = [b_s[0, SB * j + SB - 1:SB * j + SB, ls] for j in range(nb)]
    lhs = []
    for j in range(nb - 1):
        for i in range(j + 1, nb):
            lhs.append(qq[i] * jnp.exp2(bq[i] - bend[j]))
    prod = []
    for i in range(nb):
        for s in range(SB):
            r = SB * i + s
            brow = b_s[0, r:r + 1, ls]
            krow = b_s[1, r:r + 1, ls]
            arg = bq[i] - brow
            if s > 0:
                arg = jnp.minimum(arg, 0.0)
            prod.append(qq[i] * krow * jnp.exp2(arg))
    return jnp.concatenate(lhs, axis=0).astype(BF16), jnp.concatenate(prod, axis=0).astype(BF16)


def _hgrn_score_stage(scaled, kt_s, r0, ls, ones_r, lane, lane_blk, causal):
    C, SB = HG_CHUNK, HG_SUB
    nb = C // SB
    lhs, prod = scaled
    res = _nt(lhs, kt_s[pl.ds(r0, C), ls])
    red = _mm(prod, ones_r)
    srows = []
    for i in range(nb):
        s_i = jnp.zeros((SB, C), F32)
        for j in range(i):
            off = _pair_offset(j, i)
            s_i = jnp.where(lane_blk == j, res[off:off + SB], s_i)
        for s in range(SB):
            r = SB * i + s
            s_i = jnp.where(lane == r, red[SB * r:SB * (r + 1)], s_i)
        srows.append(s_i)
    return jnp.where(causal, jnp.concatenate(srows, axis=0), 0.0).astype(BF16)


def _hgrn_output_stage(scores, v_s, qd_s, kd_s, dec_s, st_s, hidx, c, r0, ls):
    rows = pl.ds(r0, HG_CHUNK)
    v = v_s[rows, ls]
    st = st_s[hidx]
    o = _nt(qd_s[rows, ls], st.astype(BF16)) + _mm(scores, v)
    st_s[hidx] = st * dec_s[c, 0:1, ls] + _tn(v, kd_s[rows, ls])
    return o


def _hgrn_body(lb_row, x_ref, nw_ref, sc_ref, sh_ref, g1_ref, wq_ref, wf_ref, wi_ref, wg_ref, wout_ref,
               gn_ref, lb_ref, o_ref, h_s, q_s, k_s, b2_s, v_s, qd_s, kt_s, kd_s, dec_s, b_s, sc_s, oh_s, og_s,
               st_s):
    si = pl.program_id(1)
    hp = pl.program_id(2)
    ts = x_ref.shape[1]
    width = q_s.shape[1]
    hpb = width // HG_DK
    C, SB = HG_CHUNK, HG_SUB

    @pl.when(si == 0)
    def _():
        for hh in range(hpb):
            st_s[hp * hpb + hh] = jnp.zeros((HG_DK, HG_DK), F32)

    @pl.when(hp == 0)
    def _():
        h_s[...] = _norm_mod(x_ref[0], nw_ref[...], sc_ref[0], sh_ref[0]).astype(BF16)

    h = h_s[...]
    lbraw = lb_ref[...]
    e = jnp.exp(lbraw - jnp.max(lbraw, axis=0, keepdims=True))
    lb = jnp.sum(e[:lb_row + 1], axis=0, keepdims=True) / jnp.sum(e, axis=0, keepdims=True)
    q_s[...] = _mm(h, wq_ref[...])
    f = _mm(h, wf_ref[...])
    ef = jnp.exp(-jnp.abs(f))
    rcp = 1.0 / (1.0 + ef)
    pos = f >= 0
    sig_p = jnp.where(pos, rcp, ef * rcp)
    sig_n = jnp.where(pos, ef * rcp, rcp)
    lf2 = jnp.log(lb + (1.0 - lb) * sig_p) * LOG2_E
    k_s[...] = (1.0 - lb) * sig_n
    v_s[...] = _mm(h, wi_ref[...]).astype(BF16)

    ri = lax.broadcasted_iota(jnp.int32, (C, C), 0)
    ci = lax.broadcasted_iota(jnp.int32, (C, C), 1)
    causal = ri >= ci
    tril = jnp.where(causal, 1.0, 0.0).astype(BF16)
    hi = lf2.astype(BF16)
    r1 = lf2 - hi.astype(F32)
    mid = r1.astype(BF16)
    lo = (r1 - mid.astype(F32)).astype(BF16)
    pieces = jnp.concatenate([hi, mid, lo], axis=1)
    for c in range(ts // C):
        rows = slice(c * C, (c + 1) * C)
        cs = _mm(tril, pieces[rows])
        b = cs[:, :width] + cs[:, width:2 * width] + cs[:, 2 * width:]
        bend = jnp.concatenate([jnp.broadcast_to(b[j + SB - 1:j + SB], (SB, width)) for j in range(0, C, SB)],
                               axis=0)
        blast = jnp.broadcast_to(b[C - 1:C], (C, width))
        k = k_s[rows]
        b2_s[rows] = b
        qd_s[rows] = (q_s[rows] * jnp.exp2(b)).astype(BF16)
        kt_s[rows] = (k * jnp.exp2(bend - b)).astype(BF16)
        kd_s[rows] = (k * jnp.exp2(blast - b)).astype(BF16)
        dec_s[c] = jnp.exp2(blast[0:SB])

    ones_r = jnp.ones((HG_DK, C), BF16)
    lane = lax.broadcasted_iota(jnp.int32, (HG_SUB, C), 1)
    lane_blk = lane // HG_SUB

    lanes = lambda hh: slice(hh * HG_DK, (hh + 1) * HG_DK)

    def scores_of(c):
        r0 = c * C if isinstance(c, int) else pl.multiple_of(c * C, C)
        for hh in range(hpb):
            scaled = _hgrn_scale_stage(q_s, k_s, b2_s, b_s, r0, lanes(hh))
            sc_s[c % 2, hh] = _hgrn_score_stage(scaled, kt_s, r0, lanes(hh), ones_r, lane, lane_blk, causal)

    def outputs_of(c):
        r0 = c * C if isinstance(c, int) else pl.multiple_of(c * C, C)
        for hh in range(hpb):
            oh_s[pl.ds(r0, C), lanes(hh)] = _hgrn_output_stage(
                sc_s[c % 2, hh], v_s, qd_s, kd_s, dec_s, st_s, hp * hpb + hh, c, r0, lanes(hh))

    nchunk = ts // C
    scores_of(0)

    def chunk(c, carry):
        scores_of(c + 1)
        outputs_of(c)
        return carry

    lax.fori_loop(0, nchunk - 1, chunk, 0)
    outputs_of(nchunk - 1)

    g = _mm(h, wg_ref[...])
    gate = g * _sigmoid(g)
    gn = gn_ref[...]
    parts = []
    for hh in range(hpb):
        ls = slice(hh * HG_DK, (hh + 1) * HG_DK)
        oh = oh_s[:, ls]
        ms = jnp.mean(oh * oh, axis=-1, keepdims=True)
        parts.append(oh * lax.rsqrt(ms + EPS) * gn)
    og_s[hp] = (jnp.concatenate(parts, axis=1) * gate).astype(BF16)

    nblk = og_s.shape[0]

    @pl.when(hp == nblk - 1)
    def _():
        y = jnp.zeros(o_ref.shape[1:], F32)
        for blk in range(nblk):
            y = y + _mm(og_s[blk], wout_ref[blk * width:(blk + 1) * width, :])
        o_ref[0] = x_ref[0] + g1_ref[0] * y


def _hgrn_layer(x, nw, sc, sh, g1, w_in, w_out, gnorm, hg_lb, lb_row):
    bsz, seq, d = x.shape
    ts = min(512, seq)
    hpb = HG_HEADS
    width = hpb * HG_DK
    nblk = d // width
    w_in = w_in.astype(BF16)
    seg = lambda k: pl.BlockSpec((d, width), lambda b, s, hp, k=k: (0, k * nblk + hp))
    vec = pl.BlockSpec((1, 1, d), lambda b, s, hp: (b, 0, 0))
    return pl.pallas_call(
        functools.partial(_hgrn_body, lb_row),
        grid=(bsz, seq // ts, nblk),
        in_specs=[pl.BlockSpec((1, ts, d), lambda b, s, hp: (b, s, 0)),
                  pl.BlockSpec((1, d), lambda b, s, hp: (0, 0)),
                  vec, vec, vec,
                  seg(0), seg(1), seg(2), seg(3),
                  pl.BlockSpec((d, d), lambda b, s, hp: (0, 0)),
                  pl.BlockSpec((1, HG_DK), lambda b, s, hp: (0, 0)),
                  pl.BlockSpec((hg_lb.shape[0], width), lambda b, s, hp: (0, hp))],
        out_specs=pl.BlockSpec((1, ts, d), lambda b, s, hp: (b, s, 0)),
        out_shape=jax.ShapeDtypeStruct(x.shape, F32),
        scratch_shapes=[pltpu.VMEM((ts, d), BF16),
                        pltpu.VMEM((ts, width), F32), pltpu.VMEM((ts, width), F32),
                        pltpu.VMEM((ts, width), F32), pltpu.VMEM((ts, width), BF16),
                        pltpu.VMEM((ts, width), BF16), pltpu.VMEM((ts, width), BF16),
                        pltpu.VMEM((ts, width), BF16),
                        pltpu.VMEM((ts // HG_CHUNK, HG_SUB, width), F32),
                        pltpu.VMEM((2, HG_CHUNK, width), F32),
                        pltpu.VMEM((2, hpb, HG_CHUNK, HG_CHUNK), BF16),
                        pltpu.VMEM((ts, width), F32),
                        pltpu.VMEM((nblk, ts, width), BF16),
                        pltpu.VMEM((HG_HEADS, HG_DK, HG_DK), F32)],
        compiler_params=_params(("arbitrary", "arbitrary", "arbitrary")),
        name="hgrn2_layer",
    )(x, nw.reshape(1, d), sc, sh, g1, w_in, w_in, w_in, w_in, w_out.astype(BF16),
      gnorm.reshape(1, HG_DK), hg_lb)


def _ffn_body(final, nchunk, x_ref, xh_ref, nw_ref, sc_ref, sh_ref, g2_ref, wa_ref, wv_ref, cw_ref, cb_ref,
              wd_ref, fn_ref, o_ref, hx_s, a_s):
    i = pl.program_id(1)
    tm = x_ref.shape[1]
    H = FFN_HALO
    x = x_ref[0]
    nw, sc, sh = nw_ref[...], sc_ref[0], sh_ref[0]
    hx_s[H:, :] = _norm_mod(x, nw, sc, sh).astype(BF16)
    hx_s[0:H, :] = jnp.where(i > 0, _norm_mod(xh_ref[0], nw, sc, sh), 0.0).astype(BF16)
    ff = wa_ref.shape[1]
    fc = ff // nchunk
    y = jnp.zeros((tm, x.shape[1]), F32)
    for c in range(nchunk):
        cols = slice(c * fc, (c + 1) * fc)
        a_s[...] = _mm(hx_s[...], wa_ref[:, cols])
        v = _mm(hx_s[H:, :], wv_ref[:, cols])
        cw = cw_ref[:, cols]
        conv = (cw[0:1] * a_s[H - 2:H - 2 + tm, :] + cw[1:2] * a_s[H - 1:H - 1 + tm, :]
                + cw[2:3] * a_s[H:H + tm, :] + cb_ref[:, cols])
        u = (conv * _sigmoid(conv) * v).astype(BF16)
        y = y + _mm(u, wd_ref[cols, :])
    out = x + g2_ref[0] * y
    if final:
        ms = jnp.mean(out * out, axis=-1, keepdims=True)
        out = out * lax.rsqrt(ms + EPS) * fn_ref[...]
    o_ref[0] = out


def _ffn_layer(x, nw, sc, sh, g2, w_up, conv_w, conv_b, w_down, final_norm, final):
    bsz, seq, d = x.shape
    ff = w_down.shape[0]
    tm = min(512, seq)
    nchunk = 1
    H = FFN_HALO
    wa = w_up[:, :ff].astype(BF16)
    wv = w_up[:, ff:].astype(BF16)
    vec = pl.BlockSpec((1, 1, d), lambda b, i: (b, 0, 0))
    full = lambda shape: pl.BlockSpec(shape, lambda b, i: tuple(0 for _ in shape),
                                      pipeline_mode=pl.Buffered(1))
    return pl.pallas_call(
        functools.partial(_ffn_body, final, nchunk),
        grid=(bsz, seq // tm),
        in_specs=[pl.BlockSpec((1, tm, d), lambda b, i: (b, i, 0)),
                  pl.BlockSpec((1, H, d), lambda b, i: (b, jnp.maximum(i * (tm // H) - 1, 0), 0)),
                  full((1, d)), vec, vec, vec,
                  full((d, ff)), full((d, ff)), full((CONV_W, ff)), full((1, ff)), full((ff, d)),
                  full((1, d))],
        out_specs=pl.BlockSpec((1, tm, d), lambda b, i: (b, i, 0)),
        out_shape=jax.ShapeDtypeStruct(x.shape, F32),
        scratch_shapes=[pltpu.VMEM((tm + H, d), BF16), pltpu.VMEM((tm + H, ff // nchunk), F32)],
        compiler_params=_params(("arbitrary", "arbitrary")),
        name="conv_ffn_layer",
    )(x, x, nw.reshape(1, d), sc, sh, g2, wa, wv, conv_w, conv_b.reshape(1, ff), w_down.astype(BF16),
      final_norm.reshape(1, d))


def _alibi_slopes():
    return [[2.0 ** (-8.0 * (g * NSA_GROUP + h + 1) / NSA_HEADS) for h in range(NSA_GROUP)]
            for g in range(NSA_KV)]


def _slot(lane, g):
    return (lane >= g * NSA_HD) & (lane < (g + 1) * NSA_HD)


def _nsa_in_body(x_ref, nw_ref, sc_ref, sh_ref, w_ref, fks_ref, fkw_ref, fv_ref,
                 q_ref, kvc_ref, ks_ref, vs_ref, kw_ref, vw_ref, gate_ref):
    d = x_ref.shape[2]
    h = _norm_mod(x_ref[0], nw_ref[...], sc_ref[0], sh_ref[0]).astype(BF16)
    p = _mm(h, w_ref[...])
    q_ref[0] = (p[:, :d] * (NSA_HD ** -0.5)).astype(BF16)
    o = d
    for i in range(kvc_ref.shape[1]):
        kvc_ref[0, i] = p[:, o + i * 2 * NSA_HD:o + (i + 1) * 2 * NSA_HD]
    o += 2 * KVW
    ksb, vsb, kwb, vwb = [p[:, o + i * KVW:o + (i + 1) * KVW].astype(BF16) for i in range(4)]
    o += 4 * KVW
    lane = lax.broadcasted_iota(jnp.int32, (1, KVW), 1)
    lane_v = lax.broadcasted_iota(jnp.int32, (1, 2 * NSA_HD), 1)
    for g in range(NSA_KV):
        own = _slot(lane, g)
        ks_ref[0, g] = jnp.where(own, ksb, fks_ref[g])
        kw_ref[0, g] = jnp.where(own, kwb, fkw_ref[g])
        half = slice((g // 2) * 2 * NSA_HD, (g // 2 + 1) * 2 * NSA_HD)
        own_v = _slot(lane_v, g % 2)
        fv = fv_ref[g].astype(BF16)
        vs_ref[0, g] = jnp.where(own_v, vsb[:, half], fv)
        vw_ref[0, g] = jnp.where(own_v, vwb[:, half], fv)
    gate_ref[0] = _sigmoid(p[:, o:])


def _nsa_in(x, nw, sc, sh, w_cat, fks, fkw, fv):
    bsz, seq, d = x.shape
    tm = min(512, seq)
    ncol = w_cat.shape[1]
    G = NSA_KV
    tile = lambda w: pl.BlockSpec((1, tm, w), lambda b, i: (b, i, 0))
    gtile = lambda w: pl.BlockSpec((1, G, tm, w), lambda b, i: (b, 0, i, 0))
    vec = pl.BlockSpec((1, 1, d), lambda b, i: (b, 0, 0))
    sds = lambda w, dt: jax.ShapeDtypeStruct((bsz, seq, w), dt)
    gsds = lambda w: jax.ShapeDtypeStruct((bsz, G, seq, w), BF16)
    feat = pl.BlockSpec((G, tm, KVW), lambda b, i: (0, i, 0))
    return pl.pallas_call(
        _nsa_in_body,
        grid=(bsz, seq // tm),
        in_specs=[tile(d), pl.BlockSpec((1, d), lambda b, i: (0, 0)), vec, vec,
                  pl.BlockSpec((d, ncol), lambda b, i: (0, 0)), feat, feat,
                  pl.BlockSpec((G, 1, 2 * NSA_HD), lambda b, i: (0, 0, 0))],
        out_specs=[tile(d), gtile(2 * NSA_HD), gtile(KVW), gtile(2 * NSA_HD), gtile(KVW), gtile(2 * NSA_HD),
                   tile(128)],
        out_shape=[sds(d, BF16), jax.ShapeDtypeStruct((bsz, G, seq, 2 * NSA_HD), F32), gsds(KVW),
                   gsds(2 * NSA_HD), gsds(KVW), gsds(2 * NSA_HD), sds(128, F32)],
        compiler_params=_params(("arbitrary", "arbitrary")),
        name="nsa_in_proj",
    )(x, nw.reshape(1, d), sc, sh, w_cat, fks, fkw, fv)


def _cmp_body(kvc_ref, pe_ref, w1_ref, w2_ref, mid_ref, kc_ref, vc_ref):
    nseg = kc_ref.shape[2]
    PW = 2 * NSA_HD
    row = lax.broadcasted_iota(jnp.int32, (nseg, PW), 0)
    lane = lax.broadcasted_iota(jnp.int32, (1, KVW), 1)
    lane_v = lax.broadcasted_iota(jnp.int32, (1, PW), 1)
    outs = []
    for i in range(2):
        pairs = []
        for pair in range(NSA_KV // 2):
            x = jnp.concatenate([kvc_ref[0, 2 * i + pair, pl.ds(l, nseg, stride=CMP_STRIDE), :]
                                 for l in range(CMP_STRIDE)], axis=1)
            a = _mm((x + pe_ref[i, 0]).astype(BF16), w1_ref[i, 0])
            bm = _mm((x + pe_ref[i, 1]).astype(BF16), w1_ref[i, 1])
            pre = a + pltpu.roll(bm, nseg - 1, 0)
            act = pre * _sigmoid(pre)
            out = _mm(act.astype(BF16), w2_ref[i])
            pairs.append(jnp.where(row < nseg - 1, out, 0.0))
        outs.append(pairs)
    kc_all = jnp.concatenate(outs[0], axis=1).astype(BF16)
    for g in range(NSA_KV):
        kc_ref[0, g] = jnp.where(_slot(lane, g), kc_all, mid_ref[g])
        vc_ref[0, g] = jnp.where(_slot(lane_v, g % 2), outs[1][g // 2], 0.0).astype(BF16)


def _compress(kvc, pe, w1, w2, mid):
    bsz, nwin, seq, width = kvc.shape
    nseg = seq // CMP_STRIDE
    G = NSA_KV
    full = lambda a: pl.BlockSpec(a.shape, lambda b: tuple(0 for _ in a.shape))
    return pl.pallas_call(
        _cmp_body,
        grid=(bsz,),
        in_specs=[pl.BlockSpec((1, nwin, seq, width), lambda b: (b, 0, 0, 0)), full(pe), full(w1), full(w2),
                  full(mid)],
        out_specs=[pl.BlockSpec((1, G, nseg, KVW), lambda b: (b, 0, 0, 0)),
                   pl.BlockSpec((1, G, nseg, 2 * NSA_HD), lambda b: (b, 0, 0, 0))],
        out_shape=[jax.ShapeDtypeStruct((bsz, G, nseg, KVW), BF16),
                   jax.ShapeDtypeStruct((bsz, G, nseg, 2 * NSA_HD), BF16)],
        compiler_params=_params(("arbitrary",)),
        name="nsa_compress",
    )(kvc, pe, w1, w2, mid)


def _assemble_heads(parts):
    tq = parts[0][0].shape[0]
    first = lax.broadcasted_iota(jnp.int32, (1, 2 * NSA_HD), 1) < NSA_HD
    cols = []
    for h in range(NSA_GROUP):
        for pair in range(NSA_KV // 2):
            cols.append(jnp.where(first, parts[2 * pair][h], parts[2 * pair + 1][h]))
    return jnp.concatenate(cols, axis=1)


def _sel_body(nsel, q_ref, kc_ref, vc_ref, sf_ref, ovt_ref, oc_ref, selm_ref, flag_ref):
    qi = pl.program_id(1)
    tq = q_ref.shape[1]
    n = kc_ref.shape[2]
    ns = ovt_ref.shape[0]
    q0 = qi * tq
    SB = 8
    tcol = q0 + lax.broadcasted_iota(jnp.int32, (tq, 1), 0)
    ncol = lax.broadcasted_iota(jnp.int32, (1, n), 1)
    valid = (ncol * CMP_STRIDE + (CMP_LEN - 1)) <= tcol
    rowok = jnp.where(tcol >= CMP_LEN - 1, 1.0, 0.0)
    lane = lax.broadcasted_iota(jnp.int32, (1, KVW), 1)
    jidx = lax.broadcasted_iota(jnp.int32, (ns, tq), 0)
    jloc = lax.broadcasted_iota(jnp.int32, (SB, tq), 0)
    tl = q0 + lax.broadcasted_iota(jnp.int32, (ns, tq), 1)
    cur = tl // SEL_LEN
    valid_s = jidx * SEL_LEN <= tl
    forced = ((jidx == 0) | (jidx == cur) | (jidx == cur - 1)) & valid_s
    parts = [[None] * NSA_GROUP for _ in range(NSA_KV)]
    mask_rows = [None] * NSA_KV
    for g in range(NSA_KV):
        own = _slot(lane, g)
        psum = jnp.zeros((tq, n), F32)
        ps = []
        for h in range(NSA_GROUP):
            qe = jnp.where(own, q_ref[0, :, h * KVW:(h + 1) * KVW], sf_ref[g * NSA_GROUP + h].astype(BF16))
            s = jnp.where(valid, _nt(qe, kc_ref[0, g]), NEG_INF)
            ex = jnp.exp(s - jnp.max(s, axis=-1, keepdims=True))
            p = ex * (rowok / jnp.sum(ex, axis=-1, keepdims=True))
            psum = psum + p
            ps.append(p.astype(BF16))
        og = _mm(jnp.concatenate(ps, axis=0), vc_ref[0, g])
        for h in range(NSA_GROUP):
            parts[g][h] = og[h * tq:(h + 1) * tq]
        hi = psum.astype(BF16)
        r1 = psum - hi.astype(F32)
        mid = r1.astype(BF16)
        lo = (r1 - mid.astype(F32)).astype(BF16)
        imp3 = _nt(ovt_ref[...], jnp.concatenate([hi, mid, lo], axis=0))
        imp = imp3[:, :tq] + imp3[:, tq:2 * tq] + imp3[:, 2 * tq:]
        score = jnp.where(forced, FORCED_SCORE, jnp.where(valid_s, imp, -1.0))
        blocks = [score[SB * k:SB * (k + 1)] for k in range(ns // SB)]
        cnts = [jnp.zeros((SB, tq), F32) for _ in blocks]
        for jp in range(ns):
            row = score[jp:jp + 1, :]
            for k, blk in enumerate(blocks):
                if SB * k > jp:
                    beats = jnp.where(row >= blk, 1.0, 0.0)
                elif SB * (k + 1) <= jp:
                    beats = jnp.where(row > blk, 1.0, 0.0)
                else:
                    beats = jnp.where(jloc > jp - SB * k, jnp.where(row >= blk, 1.0, 0.0),
                                      jnp.where(row > blk, 1.0, 0.0))
                cnts[k] = cnts[k] + beats
        cnt = jnp.concatenate(cnts, axis=0)
        rows = jnp.where(cnt < nsel, 0.0, NEG_INF)
        if ns < NSA_HD:
            rows = jnp.concatenate([rows, jnp.zeros((NSA_HD - ns, tq), F32)], axis=0)
        mask_rows[(g + 1) % NSA_KV] = rows
        anyq = jnp.max(jnp.where(cnt < nsel, 1.0, 0.0), axis=1, keepdims=True)
        bpt = ns // flag_ref.shape[3]
        flag_ref[0, 0, g] = jnp.concatenate(
            [jnp.broadcast_to(jnp.max(anyq[bpt * kt:bpt * (kt + 1)], axis=0, keepdims=True), (1, 128))
             for kt in range(flag_ref.shape[3])], axis=0)
    selm_ref[0] = jnp.concatenate(mask_rows, axis=0).T.astype(BF16)
    oc_ref[0] = _assemble_heads(parts)


def _att_tiles(seq):
    return min(256, seq), min(512, seq)


def _select(q, kc, vc, sf, ovt):
    bsz, seq, d = q.shape
    n = kc.shape[2]
    ns = ovt.shape[0]
    tq, tks = _att_tiles(seq)
    nkt = seq // tks
    nsel = min(SEL_TOPK, ns)
    assert ns % 8 == 0 and ns % nkt == 0
    return pl.pallas_call(
        functools.partial(_sel_body, nsel),
        grid=(bsz, seq // tq),
        in_specs=[pl.BlockSpec((1, tq, d), lambda b, i: (b, i, 0)),
                  pl.BlockSpec((1, NSA_KV, n, KVW), lambda b, i: (b, 0, 0, 0)),
                  pl.BlockSpec((1, NSA_KV, n, 2 * NSA_HD), lambda b, i: (b, 0, 0, 0)),
                  pl.BlockSpec(sf.shape, lambda b, i: (0, 0, 0)),
                  pl.BlockSpec((ns, n), lambda b, i: (0, 0))],
        out_specs=[pl.BlockSpec((1, tq, d), lambda b, i: (b, i, 0)),
                   pl.BlockSpec((1, tq, KVW), lambda b, i: (b, i, 0)),
                   pl.BlockSpec((1, 1, NSA_KV, nkt, 128), lambda b, i: (b, i, 0, 0, 0))],
        out_shape=[jax.ShapeDtypeStruct((bsz, seq, d), F32),
                   jax.ShapeDtypeStruct((bsz, seq, KVW), BF16),
                   jax.ShapeDtypeStruct((bsz, seq // tq, NSA_KV, nkt, 128), F32)],
        compiler_params=_params(("arbitrary", "arbitrary")),
        name="nsa_compressed_select",
    )(q, kc, vc, sf, ovt)


def _softmax_tile(s, vb, mask, tq, m_s, acc_s):
    tk = vb.shape[0]
    ps = []
    for h in range(NSA_GROUP):
        rows = slice(h * tq, (h + 1) * tq)
        sh = s[rows]
        if mask is not None:
            sh = jnp.where(mask, sh, NEG_INF)
        m_old = m_s[rows]
        m_new = jnp.maximum(m_old, jnp.max(sh, axis=-1, keepdims=True))
        p = jnp.exp(sh - jnp.concatenate([m_new] * (tk // 128), axis=1))
        m_s[rows] = m_new
        acc_s[rows] = acc_s[rows] * jnp.exp(m_old - m_new)
        ps.append(p.astype(BF16))
    acc_s[...] += _mm(jnp.concatenate(ps, axis=0), vb)


def _finish(g, acc_s):
    one = ((g + 1) % 2) * NSA_HD
    acc = acc_s[...]
    return acc * (1.0 / acc[:, one:one + 1])


def _att_body(tks, flag_ref, q_ref, ks_ref, vs_ref, kw_refs, vw_refs, selm_ref, sf_ref, oc_ref, gate_ref,
              gexp_ref, x_ref, g1_ref, wout_ref, o_ref, m_s, acc_s, sd_s, sw_s):
    qi = pl.program_id(1)
    tq = q_ref.shape[1]
    q0 = qi * tq
    nkt = ks_ref.shape[2] // tks
    flag0 = (pl.program_id(0) * pl.num_programs(1) + qi) * (NSA_KV * nkt)
    nwt = len(kw_refs) - 1
    lane = lax.broadcasted_iota(jnp.int32, (1, KVW), 1)
    lane_v = lax.broadcasted_iota(jnp.int32, (1, 2 * NSA_HD), 1)
    ql = lax.broadcasted_iota(jnp.int32, (tq, 1), 0)
    colw = lax.broadcasted_iota(jnp.int32, (1, (nwt + 1) * tq), 1)
    rel = jnp.where(colw // tq + qi >= nwt, colw - nwt * tq - ql, 1)
    wmask = (rel <= 0) & (rel > -WINDOW)
    selm = selm_ref[0]
    o_sel = [[None] * NSA_GROUP for _ in range(NSA_KV)]
    o_win = [[None] * NSA_GROUP for _ in range(NSA_KV)]

    def reset():
        m_s[...] = jnp.full(m_s.shape, NEG_INF, F32)
        acc_s[...] = jnp.zeros(acc_s.shape, F32)

    last = (q0 + tq + tks - 1) // tks - 1
    diag = pl.ds(pl.multiple_of(last * tks, tks), tks)
    dmask = (lax.broadcasted_iota(jnp.int32, (1, tks), 1) + last * tks) <= (ql + q0)

    def queries(g):
        own = _slot(lane, g)
        pick = _slot(lane, (g + 1) % NSA_KV)
        return jnp.concatenate(
            [jnp.where(own, q_ref[0, :, h * KVW:(h + 1) * KVW],
                       jnp.where(pick, selm, sf_ref[g * NSA_GROUP + h].astype(BF16)))
             for h in range(NSA_GROUP)], axis=0)

    def static_scores(g, qg):
        sd_s[g % 2] = _nt(qg, ks_ref[0, g, diag, :])
        sw_s[g % 2] = _nt(qg, jnp.concatenate([r[0, g] for r in kw_refs], axis=0))

    def static_outputs(g):
        _softmax_tile(sd_s[g % 2], vs_ref[0, g, diag, :], dmask, tq, m_s, acc_s)
        osg = _finish(g, acc_s)
        ps = []
        for h in range(NSA_GROUP):
            sh = jnp.where(wmask, sw_s[g % 2, h * tq:(h + 1) * tq], NEG_INF)
            ps.append(jnp.exp(sh - jnp.max(sh, axis=-1, keepdims=True)).astype(BF16))
        acc = _mm(jnp.concatenate(ps, axis=0), jnp.concatenate([r[0, g] for r in vw_refs], axis=0))
        one = ((g + 1) % 2) * NSA_HD
        owg = acc * (1.0 / acc[:, one:one + 1])
        for h in range(NSA_GROUP):
            o_sel[g][h] = osg[h * tq:(h + 1) * tq]
            o_win[g][h] = owg[h * tq:(h + 1) * tq]

    qg = queries(0)
    static_scores(0, qg)
    for g in range(NSA_KV):
        reset()

        def body(kt, carry, g=g, qg=qg):
            @pl.when(flag_ref[flag0 + g * nkt + kt] > 0)
            def _():
                tile = pl.ds(pl.multiple_of(kt * tks, tks), tks)
                _softmax_tile(_nt(qg, ks_ref[0, g, tile, :]), vs_ref[0, g, tile, :], None, tq, m_s, acc_s)
            return carry

        lax.fori_loop(0, last, body, 0)
        if g + 1 < NSA_KV:
            qg = queries(g + 1)
            static_scores(g + 1, qg)
        static_outputs(g)

    assemble = _assemble_heads
    gate = gate_ref[0]
    ghi = gate.astype(BF16)
    glo = (gate - ghi.astype(F32)).astype(BF16)
    ghl = jnp.concatenate([ghi, glo], axis=1)
    branches = (oc_ref[0], assemble(o_sel), assemble(o_win))
    o = jnp.zeros(branches[0].shape, F32)
    for c in range(3):
        o = o + _mm(ghl, gexp_ref[c]) * branches[c]
    y = _mm(o.astype(BF16), wout_ref[...])
    o_ref[0] = x_ref[0] + g1_ref[0] * y


def _attend(flags, q, ks, vs, kw, vw, selm, sf, oc, gate, gexp, x, g1, w_out):
    bsz, seq, d = x.shape
    G = NSA_KV
    tq, tks = _att_tiles(seq)
    nwt = WINDOW // tq
    tile = lambda w: pl.BlockSpec((1, tq, w), lambda b, i, f: (b, i, 0))
    kv = lambda w: pl.BlockSpec((1, G, seq, w), lambda b, i, f: (b, 0, 0, 0), pipeline_mode=pl.Buffered(1))
    wtile = lambda w, dj: pl.BlockSpec((1, G, tq, w), lambda b, i, f: (b, 0, jnp.maximum(i - dj, 0), 0))
    full = lambda a: pl.BlockSpec(a.shape, lambda b, i, f: tuple(0 for _ in a.shape))
    kw_specs = [wtile(KVW, dj) for dj in range(nwt, -1, -1)]
    vw_specs = [wtile(2 * NSA_HD, dj) for dj in range(nwt, -1, -1)]

    def body(flag_ref, q_ref, ks_ref, vs_ref, *rest):
        kw_refs = rest[:nwt + 1]
        vw_refs = rest[nwt + 1:2 * nwt + 2]
        _att_body(tks, flag_ref, q_ref, ks_ref, vs_ref, kw_refs, vw_refs, *rest[2 * nwt + 2:])

    return pl.pallas_call(
        body,
        grid_spec=pltpu.PrefetchScalarGridSpec(
            num_scalar_prefetch=1,
            grid=(bsz, seq // tq),
            in_specs=[tile(d), kv(KVW), kv(2 * NSA_HD), *kw_specs, *vw_specs, tile(KVW), full(sf), tile(d),
                      tile(128), full(gexp), tile(d), pl.BlockSpec((1, 1, d), lambda b, i, f: (b, 0, 0)),
                      full(w_out)],
            out_specs=tile(d),
            scratch_shapes=[pltpu.VMEM((NSA_GROUP * tq, 128), F32),
                            pltpu.VMEM((NSA_GROUP * tq, 2 * NSA_HD), F32),
                            pltpu.VMEM((2, NSA_GROUP * tq, tks), F32),
                            pltpu.VMEM((2, NSA_GROUP * tq, (nwt + 1) * tq), F32)]),
        out_shape=jax.ShapeDtypeStruct(x.shape, F32),
        compiler_params=_params(("arbitrary", "arbitrary")),
        name="nsa_attend",
    )(flags, q, ks, vs, *([kw] * (nwt + 1)), *([vw] * (nwt + 1)), selm, sf, oc, gate, gexp, x, g1, w_out)


def _nsa_layer(x, nw, sc, sh, g1, w_in, w_out, cmp_pe, cmp_w1, cmp_w2):
    bsz, seq, d = x.shape
    G, HPG, HD = NSA_KV, NSA_GROUP, NSA_HD
    wq = w_in[:, :d].reshape(d, G, HPG, HD).transpose(0, 2, 1, 3).reshape(d, d)
    ngate = 3 * NSA_HEADS
    wgate = jnp.pad(w_in[:, d + 6 * KVW:], ((0, 0), (0, 128 - ngate)))
    w_cat = jnp.concatenate([wq, w_in[:, d:d + 6 * KVW], wgate], axis=1).astype(BF16)
    w_out_p = w_out.reshape(G, HPG, HD, d).transpose(1, 0, 2, 3).reshape(d, d).astype(BF16)

    ns = seq // SEL_LEN
    assert ns <= HD and CMP_LEN == 2 * CMP_STRIDE
    pos = np.arange(seq)
    fks = np.zeros((G, seq, KVW), np.float32)
    fkw = np.zeros((G, seq, KVW), np.float32)
    fv = np.zeros((G, 1, 2 * HD), np.float32)
    sf = np.zeros((G * HPG, 1, KVW), np.float32)
    slopes = _alibi_slopes()
    for g in range(G):
        pick0, feat0 = ((g + 1) % G) * HD, ((g + 2) % G) * HD
        fks[g, pos, pick0 + pos // SEL_LEN] = 1.0
        for arr in (fks, fkw):
            arr[g, :, feat0 + 0] = arr[g, :, feat0 + 2] = (pos // 64) * 64
            arr[g, :, feat0 + 1] = arr[g, :, feat0 + 3] = pos % 64
        fv[g, 0, ((g + 1) % 2) * HD] = 1.0
        for h in range(HPG):
            s32 = np.float32(slopes[g][h])
            hi = s32.astype(BF16).astype(np.float32)
            lo = np.float32(s32 - hi).astype(BF16).astype(np.float32)
            sf[g * HPG + h, 0, feat0:feat0 + 4] = (hi, hi, lo, lo)
    q, kvc, ks, vs, kw, vw, gate = _nsa_in(x, nw, sc, sh, w_cat, jnp.asarray(fks, BF16),
                                           jnp.asarray(fkw, BF16), jnp.asarray(fv))

    nseg = seq // CMP_STRIDE
    eye2 = jnp.eye(2, dtype=F32)
    w1 = jnp.einsum("khlde,pq->khlpdqe", cmp_w1.reshape(2, 2, CMP_STRIDE, HD, HD), eye2).reshape(
        2, 2, CMP_STRIDE * 2 * HD, 2 * HD).astype(BF16)
    pe = jnp.broadcast_to(cmp_pe.reshape(2, 2, CMP_STRIDE, 1, HD), (2, 2, CMP_STRIDE, 2, HD)).reshape(
        2, 2, 1, CMP_STRIDE * 2 * HD)
    w2 = jnp.einsum("kde,pq->kpdqe", cmp_w2, eye2).reshape(2, 2 * HD, 2 * HD).astype(BF16)
    mid = np.zeros((G, nseg, KVW), np.float32)
    for g in range(G):
        feat0 = ((g + 2) % G) * HD
        mid[g, :, feat0 + 0] = mid[g, :, feat0 + 2] = np.arange(nseg) * CMP_STRIDE
        mid[g, :, feat0 + 1] = mid[g, :, feat0 + 3] = (CMP_LEN - 1) / 2.0
    kce, vce = _compress(kvc, pe, w1, w2, jnp.asarray(mid, BF16))

    ci = np.arange(nseg)[:, None] * CMP_STRIDE
    sj = np.arange(ns)[None, :] * SEL_LEN
    overlap = ((ci <= sj + SEL_LEN - 1) & (ci + CMP_LEN - 1 >= sj)
               & (np.arange(nseg)[:, None] < nseg - 1))
    ovt = jnp.asarray(overlap.T, BF16)
    sf = jnp.asarray(sf)
    oc, selm, tile_any = _select(q, kce, vce, sf, ovt)
    flags = (tile_any[..., 0] > 0.5).astype(jnp.int32).reshape(-1)

    gexp = np.zeros((3, 256, d), np.float32)
    for g in range(G):
        for h in range(HPG):
            for c in range(3):
                for half in range(2):
                    gexp[c, half * 128 + g * HPG * 3 + h * 3 + c, (h * G + g) * HD:(h * G + g + 1) * HD] = 1.0
    return _attend(flags, q, ks, vs, kw, vw, selm, sf, oc, gate, jnp.asarray(gexp, BF16), x, g1,
                   w_out_p)


def kernel(x, c, ada_w, ada_b, norm_mix, norm_ffn, final_norm, hg_w_in, hg_w_out, hg_gnorm, hg_lb,
           nsa_w_in, nsa_w_out, nsa_cmp_pe, nsa_cmp_w1, nsa_cmp_w2, ffn_w_up, ffn_conv_w, ffn_conv_b,
           ffn_w_down):
    depth = ada_w.shape[0]
    d = x.shape[-1]
    mod = _ada(c, ada_w, ada_b)
    for layer in range(depth):
        sh1, sc1, g1, sh2, sc2, g2 = [mod[layer, :, None, k * d:(k + 1) * d] for k in range(6)]
        j = layer // 2
        if layer % 2 == 0:
            x = _hgrn_layer(x, norm_mix[layer], sc1, sh1, g1, hg_w_in[j], hg_w_out[j], hg_gnorm[j], hg_lb, j)
        else:
            x = _nsa_layer(x, norm_mix[layer], sc1, sh1, g1, nsa_w_in[j], nsa_w_out[j], nsa_cmp_pe[j],
                           nsa_cmp_w1[j], nsa_cmp_w2[j])
        x = _ffn_layer(x, norm_ffn[layer], sc2, sh2, g2, ffn_w_up[layer], ffn_conv_w[layer],
                       ffn_conv_b[layer], ffn_w_down[layer], final_norm, layer == depth - 1)
    return x
```
